```python
import jax, jax.numpy as jnp
from jax import lax
import numpy as np

D_MODEL = 1024
BATCH = 8
SEQ = 4096
DEPTH = 2

N_A_LAYERS = DEPTH // 2
N_B_LAYERS = DEPTH - N_A_LAYERS

A_KEY_DIM = 128
A_HEADS = D_MODEL // A_KEY_DIM
A_VAL_DIM = D_MODEL // A_HEADS
A_DK = A_HEADS * A_KEY_DIM
A_DV = A_HEADS * A_VAL_DIM
A_CHUNK = 64

B_HEAD_DIM = 128
B_HEADS = D_MODEL // B_HEAD_DIM
B_WIDTH = B_HEADS * B_HEAD_DIM
B_Q_BLOCK = 128

D_FF = ((8 * D_MODEL // 3 + 127) // 128) * 128
CONV_W = 3
EPS = 1e-6
NEG_INF = -1e30

kernel_name = "yoco_hgrn2_fox_adaln_convffn"


def rms_norm(x):
    xf = x.astype(jnp.float32)
    y = xf * lax.rsqrt(jnp.mean(xf * xf, axis=-1, keepdims=True) + EPS)
    return y.astype(x.dtype)


def modulate(x, shift, scale):
    return rms_norm(x) * (1 + scale[:, None, :]) + shift[:, None, :]


def hgrn2_mixer(h, w_in, lb, norm_g, w_out):
    bsz, seq, _ = h.shape
    n_chunks = seq // A_CHUNK
    proj = (h @ w_in).astype(jnp.float32)
    q, f, i, g = jnp.split(proj, [A_DK, 2 * A_DK, 2 * A_DK + A_DV], axis=-1)
    q = jax.nn.silu(q)
    fg = lb + (1 - lb) * jax.nn.sigmoid(f)
    log_f = jnp.log(fg)
    k = 1 - fg

    def to_chunks(t, d):
        return t.reshape(bsz, n_chunks, A_CHUNK, A_HEADS, d).transpose(0, 3, 1, 2, 4)

    q, k, log_f = to_chunks(q, A_KEY_DIM), to_chunks(k, A_KEY_DIM), to_chunks(log_f, A_KEY_DIM)
    v = to_chunks(i, A_VAL_DIM)
    b = jnp.cumsum(log_f, axis=3)
    b_mid = b[:, :, :, A_CHUNK // 2:A_CHUNK // 2 + 1, :]
    q_intra = q * jnp.exp(b - b_mid)
    k_intra = k * jnp.exp(b_mid - b)
    scores = jnp.einsum('bhnck,bhnsk->bhncs', q_intra, k_intra)
    causal = jnp.tril(jnp.ones((A_CHUNK, A_CHUNK), dtype=bool))
    scores = jnp.where(causal, scores, 0.0)
    o_intra = jnp.einsum('bhncs,bhnsv->bhncv', scores, v)
    b_last = b[:, :, :, -1:, :]
    q_inter = q * jnp.exp(b)
    k_state = k * jnp.exp(b_last - b)
    chunk_decay = jnp.exp(b_last[:, :, :, 0, :])

    def step(state, xs):
        qc, kc, vc, dc = xs
        o = jnp.einsum('bhck,bhkv->bhcv', qc, state)
        state = dc[..., None] * state + jnp.einsum('bhck,bhcv->bhkv', kc, vc)
        return state, o

    xs = (jnp.moveaxis(q_inter, 2, 0), jnp.moveaxis(k_state, 2, 0),
          jnp.moveaxis(v, 2, 0), jnp.moveaxis(chunk_decay, 2, 0))
    state0 = jnp.zeros((bsz, A_HEADS, A_KEY_DIM, A_VAL_DIM), jnp.float32)
    _, o_inter = lax.scan(step, state0, xs)
    o = o_intra + jnp.moveaxis(o_inter, 0, 2)
    o = o.transpose(0, 2, 3, 1, 4).reshape(bsz, seq, A_HEADS, A_VAL_DIM)
    gate = jax.nn.silu(g.reshape(bsz, seq, A_HEADS, A_VAL_DIM))
    o = rms_norm(o) * norm_g * gate
    return o.reshape(bsz, seq, A_DV).astype(h.dtype) @ w_out


def shared_kv(x, c, kv_ada_w, kv_ada_b, kv_w, kv_b_f, k_norm_g):
    bsz, seq, _ = x.shape
    shift, scale = jnp.split(jax.nn.silu(c) @ kv_ada_w + kv_ada_b, 2, axis=-1)
    h = modulate(x, shift, scale)
    proj = h @ kv_w
    k, v, f_logit = jnp.split(proj, [B_WIDTH, 2 * B_WIDTH], axis=-1)
    k = rms_norm(k.reshape(bsz, seq, B_HEADS, B_HEAD_DIM)) * k_norm_g
    k = k.transpose(0, 2, 1, 3)
    v = v.reshape(bsz, seq, B_HEADS, B_HEAD_DIM).transpose(0, 2, 1, 3)
    log_f = jax.nn.log_sigmoid((f_logit + kv_b_f).astype(jnp.float32))
    cum_log_f = jnp.cumsum(log_f.transpose(0, 2, 1), axis=-1)
    return k, v, cum_log_f


def fox_mixer(h, k, v, cum_log_f, w_q, q_norm_g, w_out):
    bsz, seq, _ = h.shape
    n_blocks = seq // B_Q_BLOCK
    q, og = jnp.split(h @ w_q, 2, axis=-1)
    q = rms_norm(q.reshape(bsz, seq, B_HEADS, B_HEAD_DIM)) * q_norm_g * (B_HEAD_DIM ** -0.5)
    q_blocks = q.reshape(bsz, n_blocks, B_Q_BLOCK, B_HEADS, B_HEAD_DIM).transpose(1, 0, 3, 2, 4)
    f_q = cum_log_f.reshape(bsz, B_HEADS, n_blocks, B_Q_BLOCK).transpose(2, 0, 1, 3)
    starts = jnp.arange(n_blocks) * B_Q_BLOCK
    key_pos = jnp.arange(seq)

    def attend_block(args):
        qb, fq, start = args
        q_pos = start + jnp.arange(B_Q_BLOCK)
        logits = jnp.einsum('bhqd,bhkd->bhqk', qb, k).astype(jnp.float32)
        logits = logits + (fq[..., :, None] - cum_log_f[:, :, None, :])
        logits = jnp.where(key_pos[None, :] <= q_pos[:, None], logits, NEG_INF)
        p = jax.nn.softmax(logits, axis=-1)
        return jnp.einsum('bhqk,bhkd->bhqd', p.astype(v.dtype), v)

    o = lax.map(attend_block, (q_blocks, f_q, starts))
    o = o.transpose(1, 0, 3, 2, 4).reshape(bsz, seq, B_WIDTH)
    o = o * jax.nn.sigmoid(og)
    return o @ w_out


def conv_glu_ffn(h, w_up, conv_w, conv_b, w_down):
    u = h @ w_up
    u = lax.conv_general_dilated(
        u, conv_w[:, None, :], window_strides=(1,), padding=[(CONV_W - 1, 0)],
        dimension_numbers=('NWC', 'WIO', 'NWC'), feature_group_count=2 * D_FF) + conv_b
    gate, val = jnp.split(u, 2, axis=-1)
    return (jax.nn.silu(gate) * val) @ w_down


def setup_inputs(seed: int = 0) -> dict:
    key = jax.random.key(seed)
    ks = jax.random.split(key, 20)
    f32 = jnp.float32
    D = D_MODEL

    def nrm(k, shape, scale):
        return jax.random.normal(k, shape, f32) * scale

    return {
        "x": nrm(ks[0], (BATCH, SEQ, D), 1.0),
        "c": nrm(ks[1], (BATCH, D), 1.0),
        "ada_w": nrm(ks[2], (DEPTH, D, 6 * D), 0.5 * D ** -0.5),
        "ada_b": nrm(ks[3], (DEPTH, 6 * D), 0.02),
        "a_w_in": nrm(ks[4], (N_A_LAYERS, D, 2 * A_DK + 2 * A_DV), D ** -0.5),
        "a_lb_logits": nrm(ks[5], (N_A_LAYERS + 1, A_DK), 0.1),
        "a_norm_g": 1.0 + nrm(ks[6], (N_A_LAYERS, A_VAL_DIM), 0.02),
        "a_w_out": nrm(ks[7], (N_A_LAYERS, A_DV, D), A_DV ** -0.5),
        "kv_ada_w": nrm(ks[8], (D, 2 * D), 0.5 * D ** -0.5),
        "kv_ada_b": nrm(ks[9], (2 * D,), 0.02),
        "kv_w": nrm(ks[10], (D, 2 * B_WIDTH + B_HEADS), D ** -0.5),
        "kv_b_f": 3.0 + nrm(ks[11], (B_HEADS,), 0.5),
        "k_norm_g": 1.0 + nrm(ks[12], (B_HEAD_DIM,), 0.02),
        "b_w_q": nrm(ks[13], (N_B_LAYERS, D, 2 * B_WIDTH), D ** -0.5),
        "q_norm_g": 1.0 + nrm(ks[14], (N_B_LAYERS, B_HEAD_DIM), 0.02),
        "b_w_out": nrm(ks[15], (N_B_LAYERS, B_WIDTH, D), B_WIDTH ** -0.5),
        "ffn_w_up": nrm(ks[16], (DEPTH, D, 2 * D_FF), D ** -0.5),
        "ffn_conv_w": nrm(ks[17], (DEPTH, CONV_W, 2 * D_FF), CONV_W ** -0.5),
        "ffn_conv_b": nrm(ks[18], (DEPTH, 2 * D_FF), 0.02),
        "ffn_w_down": nrm(ks[19], (DEPTH, D_FF, D), D_FF ** -0.5),
    }


def reference(x, c, ada_w, ada_b, a_w_in, a_lb_logits, a_norm_g, a_w_out,
              kv_ada_w, kv_ada_b, kv_w, kv_b_f, k_norm_g,
              b_w_q, q_norm_g, b_w_out,
              ffn_w_up, ffn_conv_w, ffn_conv_b, ffn_w_down):
    lb_all = jnp.cumsum(jax.nn.softmax(a_lb_logits.astype(jnp.float32), axis=0), axis=0)[:N_A_LAYERS]
    c_act = jax.nn.silu(c)
    k_sh = v_sh = cum_log_f = None
    for l in range(DEPTH):
        mod = c_act @ ada_w[l] + ada_b[l]
        sh1, sc1, g1, sh2, sc2, g2 = jnp.split(mod, 6, axis=-1)
        if l == N_A_LAYERS:
            k_sh, v_sh, cum_log_f = shared_kv(x, c, kv_ada_w, kv_ada_b, kv_w, kv_b_f, k_norm_g)
        h = modulate(x, sh1, sc1)
        if l < N_A_LAYERS:
            y = hgrn2_mixer(h, a_w_in[l], lb_all[l], a_norm_g[l], a_w_out[l])
        else:
            j = l - N_A_LAYERS
            y = fox_mixer(h, k_sh, v_sh, cum_log_f, b_w_q[j], q_norm_g[j], b_w_out[j])
        x = x + g1[:, None, :] * y
        h = modulate(x, sh2, sc2)
        x = x + g2[:, None, :] * conv_glu_ffn(h, ffn_w_up[l], ffn_conv_w[l], ffn_conv_b[l], ffn_w_down[l])
    return x
```

```python
import functools

import jax
import jax.numpy as jnp
from jax import lax
from jax.experimental import pallas as pl
from jax.experimental.pallas import tpu as pltpu

F32 = jnp.float32
BF16 = jnp.bfloat16

EPS = 1e-6
NEG_INF = -1e30
HEAD_DIM = 128
HGRN_CHUNK = 64
CONV_W = 3

V7X_VMEM_BYTES = 64 * 1024 * 1024
VMEM_LIMIT_BYTES = V7X_VMEM_BYTES - 8 * 1024 * 1024

MOD_TN = 512
HGRN_TS = 256
FFN_TM = 512
FFN_FC = 256
KV_TS = 512
ATT_TQ = 256


def _dot(a, b):
    return jnp.dot(a, b, preferred_element_type=F32)


def _dot_nt(a, b):
    return lax.dot_general(a, b, (((1,), (1,)), ((), ())), preferred_element_type=F32)


def _dot_tn(a, b):
    return lax.dot_general(a, b, (((0,), (0,)), ((), ())), preferred_element_type=F32)


def _sigmoid(x):
    return 1.0 / (1.0 + jnp.exp(-x))


def _rms(x):
    return x * lax.rsqrt(jnp.mean(x * x, axis=-1, keepdims=True) + EPS)


def _modulate(x, shift, scale):
    return _rms(x) * (1.0 + scale) + shift


def _split_bf16(x):
    hi = x.astype(BF16)
    lo = (x - hi.astype(F32)).astype(BF16)
    return hi, lo


def _resident(block_shape, index_map):
    return pl.BlockSpec(block_shape, index_map, pipeline_mode=pl.Buffered(1))


def _params(semantics):
    return pltpu.CompilerParams(dimension_semantics=semantics, vmem_limit_bytes=VMEM_LIMIT_BYTES)


def _mod_kernel(c_ref, w_ref, b_ref, o_ref):
    c = c_ref[...]
    ca_hi, ca_lo = _split_bf16(c * _sigmoid(c))
    w_hi, w_lo = _split_bf16(w_ref[...])
    acc = _dot(ca_hi, w_hi) + _dot(ca_lo, w_hi) + _dot(ca_hi, w_lo)
    o_ref[...] = acc + b_ref[...]


def _mod_call(c, w3, b3, layer):
    bsz, d = c.shape
    n = w3.shape[-1]
    tn = MOD_TN
    return pl.pallas_call(
        _mod_kernel,
        grid=(n // tn,),
        in_specs=[
            pl.BlockSpec((bsz, d), lambda j: (0, 0)),
            pl.BlockSpec((None, d, tn), lambda j: (layer, 0, j)),
            pl.BlockSpec((None, 1, tn), lambda j: (layer, 0, j)),
        ],
        out_specs=pl.BlockSpec((bsz, tn), lambda j: (0, j)),
        out_shape=jax.ShapeDtypeStruct((bsz, n), F32),
        compiler_params=_params(("arbitrary",)),
        name="adaln_mod",
    )(c, w3, b3)


def _hgrn_kernel(x_ref, mod_ref, win_ref, wout_ref, lb_ref, ng_ref, tri_ref, o_ref,
                 proj_sc, y_sc, st_sc, *, ts, heads, layer):
    dk = HEAD_DIM
    w = heads * dk
    chunk = HGRN_CHUNK
    nch = ts // chunk

    @pl.when(pl.program_id(1) == 0)
    def _():
        st_sc[...] = jnp.zeros_like(st_sc)

    x = x_ref[0]
    mod = mod_ref[0]
    h = _modulate(x, mod[0:1], mod[1:2]).astype(BF16)
    proj_sc[...] = _dot(h, win_ref[...])

    lb_e = jnp.exp(lb_ref[...] - jnp.max(lb_ref[...], axis=0, keepdims=True))
    lb = jnp.sum(lb_e[0:layer + 1], axis=0, keepdims=True) / jnp.sum(lb_e, axis=0, keepdims=True)
    fg = lb + (1.0 - lb) * _sigmoid(proj_sc[:, w:2 * w])
    logf_hi, logf_lo = _split_bf16(jnp.log(fg))
    tri = tri_ref[...]
    bc = _dot(tri, logf_hi) + _dot(tri, logf_lo)

    def per_chunk_row(r):
        return jnp.concatenate(
            [jnp.broadcast_to(bc[c * chunk + r:c * chunk + r + 1], (chunk, w)) for c in range(nch)], axis=0)

    b_mid = per_chunk_row(chunk // 2)
    b_last = per_chunk_row(chunk - 1)

    q = proj_sc[:, 0:w]
    q = q * _sigmoid(q)
    kk = 1.0 - fg
    q_intra = (q * jnp.exp(bc - b_mid)).astype(BF16)
    k_intra = (kk * jnp.exp(b_mid - bc)).astype(BF16)
    q_inter = (q * jnp.exp(bc)).astype(BF16)
    k_state = (kk * jnp.exp(b_last - bc)).astype(BF16)
    v = proj_sc[:, 2 * w:3 * w].astype(BF16)
    g = proj_sc[:, 3 * w:4 * w]
    gate = g * _sigmoid(g) * ng_ref[...]

    row = lax.broadcasted_iota(jnp.int32, (ts, ts), 0)
    col = lax.broadcasted_iota(jnp.int32, (ts, ts), 1)
    mask = (row - col).astype(jnp.uint32) <= (row & (chunk - 1)).astype(jnp.uint32)

    for hd in range(heads):
        cs = slice(hd * dk, (hd + 1) * dk)
        s = _dot_nt(q_intra[:, cs], k_intra[:, cs])
        s = jnp.where(mask, s, 0.0).astype(BF16)
        o_intra = _dot(s, v[:, cs])
        st = st_sc[hd]
        outs = []
        for c in range(nch):
            rs = slice(c * chunk, (c + 1) * chunk)
            outs.append(o_intra[rs] + _dot_nt(q_inter[rs, cs], st.astype(BF16)))
            decay = jnp.exp(bc[(c + 1) * chunk - 1:(c + 1) * chunk, cs])
            st = decay * st + _dot_tn(v[rs, cs], k_state[rs, cs])
        st_sc[hd] = st
        o = jnp.concatenate(outs, axis=0)
        y_sc[:, cs] = (_rms(o) * gate[:, cs]).astype(BF16)

    o_ref[0] = x + mod[2:3] * _dot(y_sc[...], wout_ref[...])


def _hgrn_call(x, mod, w_in, w_out, lb_logits, layer, norm_g):
    bsz, seq, d = x.shape
    ts = HGRN_TS
    heads = w_out.shape[0] // HEAD_DIM
    w = heads * HEAD_DIM
    n_lb = lb_logits.shape[0]
    idx = jnp.arange(ts)
    tri = ((idx[:, None] >= idx[None, :]) & (idx[:, None] // HGRN_CHUNK == idx[None, :] // HGRN_CHUNK)).astype(BF16)
    kern = functools.partial(_hgrn_kernel, ts=ts, heads=heads, layer=layer)
    return pl.pallas_call(
        kern,
        grid=(bsz, seq // ts),
        in_specs=[
            pl.BlockSpec((1, ts, d), lambda b, s: (b, s, 0)),
            pl.BlockSpec((1, 6, d), lambda b, s: (b, 0, 0)),
            _resident((d, 4 * w), lambda b, s: (0, 0)),
            _resident((w, d), lambda b, s: (0, 0)),
            _resident((n_lb, w), lambda b, s: (0, 0)),
            _resident((1, w), lambda b, s: (0, 0)),
            _resident((ts, ts), lambda b, s: (0, 0)),
        ],
        out_specs=pl.BlockSpec((1, ts, d), lambda b, s: (b, s, 0)),
        out_shape=jax.ShapeDtypeStruct(x.shape, F32),
        scratch_shapes=[
            pltpu.VMEM((ts, 4 * w), F32),
            pltpu.VMEM((ts, w), BF16),
            pltpu.VMEM((heads, HEAD_DIM, HEAD_DIM), F32),
        ],
        compiler_params=_params(("arbitrary", "arbitrary")),
        name="hgrn2_layer",
    )(x, mod, w_in.astype(BF16), w_out.astype(BF16), lb_logits.astype(F32),
      jnp.tile(norm_g, heads).reshape(1, w), tri)


def _ffn_kernel(x_ref, mod_ref, wup_ref, cw_ref, cb_ref, wdn_ref, o_ref, tail_sc, *, tm, ff):
    fc = FFN_FC

    @pl.when(pl.program_id(1) == 0)
    def _():
        tail_sc[...] = jnp.zeros_like(tail_sc)

    x = x_ref[0]
    mod = mod_ref[0]
    h = _modulate(x, mod[3:4], mod[4:5]).astype(BF16)

    def conv_cols(c0):
        u = _dot(h, wup_ref[:, c0:c0 + fc])
        prev = tail_sc[:, c0:c0 + fc]
        tail_sc[:, c0:c0 + fc] = u[tm - 8:tm]
        ext = jnp.concatenate([prev, u], axis=0)
        cw = cw_ref[:, c0:c0 + fc]
        return (cw[2:3] * u + cw[1:2] * ext[7:tm + 7] + cw[0:1] * ext[6:tm + 6]
                + cb_ref[:, c0:c0 + fc])

    acc = jnp.zeros((tm, x.shape[-1]), F32)
    for j in range(ff // fc):
        gate = conv_cols(j * fc)
        val = conv_cols(ff + j * fc)
        act = (gate * _sigmoid(gate) * val).astype(BF16)
        acc = acc + _dot(act, wdn_ref[j * fc:(j + 1) * fc, :])
    o_ref[0] = x + mod[5:6] * acc


def _ffn_call(x, mod, w_up, conv_w, conv_b, w_down):
    bsz, seq, d = x.shape
    ff = w_down.shape[0]
    tm = FFN_TM
    kern = functools.partial(_ffn_kernel, tm=tm, ff=ff)
    return pl.pallas_call(
        kern,
        grid=(bsz, seq // tm),
        in_specs=[
            pl.BlockSpec((1, tm, d), lambda b, s: (b, s, 0)),
            pl.BlockSpec((1, 6, d), lambda b, s: (b, 0, 0)),
            _resident((d, 2 * ff), lambda b, s: (0, 0)),
            _resident((CONV_W, 2 * ff), lambda b, s: (0, 0)),
            _resident((1, 2 * ff), lambda b, s: (0, 0)),
            _resident((ff, d), lambda b, s: (0, 0)),
        ],
        out_specs=pl.BlockSpec((1, tm, d), lambda b, s: (b, s, 0)),
        out_shape=jax.ShapeDtypeStruct(x.shape, F32),
        scratch_shapes=[pltpu.VMEM((8, 2 * ff), F32)],
        compiler_params=_params(("arbitrary", "arbitrary")),
        name="conv_glu_ffn",
    )(x, mod, w_up.astype(BF16), conv_w, conv_b.reshape(1, 2 * ff), w_down.astype(BF16))


def _kv_kernel(x_ref, mod_ref, wkv_ref, wf_ref, bf_ref, kg_ref, tri_ref, k_ref, v_ref, nc_ref, carry_sc,
               *, ts, heads):
    dh = HEAD_DIM
    w = heads * dh

    @pl.when(pl.program_id(1) == 0)
    def _():
        carry_sc[...] = jnp.zeros_like(carry_sc)

    mod = mod_ref[0]
    h = _modulate(x_ref[0], mod[0:1], mod[1:2]).astype(BF16)
    kv = _dot(h, wkv_ref[...])
    kg = kg_ref[...]
    for hd in range(heads):
        cs = slice(hd * dh, (hd + 1) * dh)
        k_ref[0, :, cs] = (_rms(kv[:, cs]) * kg).astype(BF16)
    v_ref[0] = kv[:, w:2 * w].astype(BF16)

    z = _dot_nt(wf_ref[...], h) + bf_ref[...]
    log_f = jnp.minimum(z, 0.0) - jnp.log(1.0 + jnp.exp(-jnp.abs(z)))
    hi, lo = _split_bf16(log_f)
    tri = tri_ref[...]
    cum = _dot(hi, tri) + _dot(lo, tri) + carry_sc[:, 0:1]
    carry_sc[...] = jnp.broadcast_to(cum[:, ts - 1:ts], carry_sc.shape)
    nc_ref[0] = -cum


def _kv_call(x, mod, kv_w, kv_b_f, k_norm_g):
    bsz, seq, d = x.shape
    heads = kv_b_f.shape[0]
    w = heads * HEAD_DIM
    ts = KV_TS
    idx = jnp.arange(ts)
    tri = (idx[:, None] <= idx[None, :]).astype(BF16)
    kern = functools.partial(_kv_kernel, ts=ts, heads=heads)
    return pl.pallas_call(
        kern,
        grid=(bsz, seq // ts),
        in_specs=[
            pl.BlockSpec((1, ts, d), lambda b, s: (b, s, 0)),
            pl.BlockSpec((1, 2, d), lambda b, s: (b, 0, 0)),
            _resident((d, 2 * w), lambda b, s: (0, 0)),
            _resident((heads, d), lambda b, s: (0, 0)),
            _resident((heads, 1), lambda b, s: (0, 0)),
            _resident((1, HEAD_DIM), lambda b, s: (0, 0)),
            _resident((ts, ts), lambda b, s: (0, 0)),
        ],
        out_specs=[
            pl.BlockSpec((1, ts, w), lambda b, s: (b, s, 0)),
            pl.BlockSpec((1, ts, w), lambda b, s: (b, s, 0)),
            pl.BlockSpec((1, heads, ts), lambda b, s: (b, 0, s)),
        ],
        out_shape=[
            jax.ShapeDtypeStruct((bsz, seq, w), BF16),
            jax.ShapeDtypeStruct((bsz, seq, w), BF16),
            jax.ShapeDtypeStruct((bsz, heads, seq), F32),
        ],
        scratch_shapes=[pltpu.VMEM((heads, HEAD_DIM), F32)],
        compiler_params=_params(("arbitrary", "arbitrary")),
        name="shared_kv",
    )(x, mod, kv_w[:, :2 * w].astype(BF16), kv_w[:, 2 * w:].T.astype(BF16), kv_b_f.reshape(heads, 1),
      k_norm_g.reshape(1, HEAD_DIM), tri)


def _fox_kernel(x_ref, mod_ref, wq_ref, qg_ref, k_ref, v_ref, nc_ref, wout_ref, o_ref,
                q_sc, gate_sc, y_sc, *, tq, heads):
    dh = HEAD_DIM
    w = heads * dh
    tk = tq
    qi = pl.program_id(1)

    x = x_ref[0]
    mod = mod_ref[0]
    h = _modulate(x, mod[0:1], mod[1:2]).astype(BF16)
    pq = _dot(h, wq_ref[...])
    gate_sc[...] = _sigmoid(pq[:, w:2 * w])
    qg = qg_ref[...] * (dh ** -0.5)
    for hd in range(heads):
        cs = slice(hd * dh, (hd + 1) * dh)
        q_sc[:, cs] = (_rms(pq[:, cs]) * qg).astype(BF16)

    row = lax.broadcasted_iota(jnp.int32, (tq, tk), 0)
    col = lax.broadcasted_iota(jnp.int32, (tq, tk), 1)
    causal = col <= row

    for hd in range(heads):
        cs = slice(hd * dh, (hd + 1) * dh)
        q = q_sc[:, cs]

        def step(j, carry, masked):
            m_prev, l_prev, acc = carry
            ks = pl.multiple_of(j * tk, tk)
            k = k_ref[0, pl.ds(ks, tk), cs]
            v = v_ref[0, pl.ds(ks, tk), cs]
            s = _dot_nt(q, k) + nc_ref[0, hd, j]
            if masked:
                s = jnp.where(causal, s, NEG_INF)
            m_new = jnp.maximum(m_prev, jnp.max(s, axis=-1, keepdims=True))
            alpha = jnp.exp(m_prev - m_new)
            p = jnp.exp(s - m_new)
            l_new = alpha * l_prev + jnp.sum(p, axis=-1, keepdims=True)
            acc = alpha * acc + _dot(p.astype(BF16), v)
            return m_new, l_new, acc

        init = (jnp.full((tq, 1), NEG_INF, F32), jnp.zeros((tq, 1), F32), jnp.zeros((tq, dh), F32))
        carry = lax.fori_loop(0, qi, functools.partial(step, masked=False), init)
        _, l_fin, acc = step(qi, carry, masked=True)
        y_sc[:, cs] = (acc * (1.0 / l_fin) * gate_sc[:, cs]).astype(BF16)

    o_ref[0] = x + mod[2:3] * _dot(y_sc[...], wout_ref[...])


def _fox_call(x, mod, w_q, q_norm_g, w_out, k, v, ncum):
    bsz, seq, d = x.shape
    w = w_out.shape[0]
    heads = w // HEAD_DIM
    tq = ATT_TQ
    nk = seq // tq
    kern = functools.partial(_fox_kernel, tq=tq, heads=heads)
    return pl.pallas_call(
        kern,
        grid=(bsz, seq // tq),
        in_specs=[
            pl.BlockSpec((1, tq, d), lambda b, s: (b, s, 0)),
            pl.BlockSpec((1, 6, d), lambda b, s: (b, 0, 0)),
            _resident((d, 2 * w), lambda b, s: (0, 0)),
            _resident((1, HEAD_DIM), lambda b, s: (0, 0)),
            _resident((1, seq, w), lambda b, s: (b, 0, 0)),
            _resident((1, seq, w), lambda b, s: (b, 0, 0)),
            _resident((1, heads, nk, 1, tq), lambda b, s: (b, 0, 0, 0, 0)),
            _resident((w, d), lambda b, s: (0, 0)),
        ],
        out_specs=pl.BlockSpec((1, tq, d), lambda b, s: (b, s, 0)),
        out_shape=jax.ShapeDtypeStruct(x.shape, F32),
        scratch_shapes=[
            pltpu.VMEM((tq, w), BF16),
            pltpu.VMEM((tq, w), F32),
            pltpu.VMEM((tq, w), BF16),
        ],
        compiler_params=_params(("arbitrary", "arbitrary")),
        name="fox_layer",
    )(x, mod, w_q.astype(BF16), q_norm_g.reshape(1, HEAD_DIM), k, v,
      ncum.reshape(bsz, heads, nk, 1, tq), w_out.astype(BF16))


def kernel(x, c, ada_w, ada_b, a_w_in, a_lb_logits, a_norm_g, a_w_out, kv_ada_w, kv_ada_b, kv_w, kv_b_f, k_norm_g,
           b_w_q, q_norm_g, b_w_out, ffn_w_up, ffn_conv_w, ffn_conv_b, ffn_w_down):
    bsz, seq, d = x.shape
    depth = ada_w.shape[0]
    n_a = a_w_in.shape[0]
    ada_b3 = ada_b.reshape(depth, 1, 6 * d)
    k_sh = v_sh = ncum = None
    for l in range(depth):
        mod = _mod_call(c, ada_w, ada_b3, l).reshape(bsz, 6, d)
        if l == n_a:
            kv_mod = _mod_call(c, kv_ada_w[None], kv_ada_b.reshape(1, 1, 2 * d), 0).reshape(bsz, 2, d)
            k_sh, v_sh, ncum = _kv_call(x, kv_mod, kv_w, kv_b_f, k_norm_g)
        if l < n_a:
            x = _hgrn_call(x, mod, a_w_in[l], a_w_out[l], a_lb_logits, l, a_norm_g[l])
        else:
            j = l - n_a
            x = _fox_call(x, mod, b_w_q[j], q_norm_g[j], b_w_out[j], k_sh, v_sh, ncum)
        x = _ffn_call(x, mod, ffn_w_up[l], ffn_conv_w[l], ffn_conv_b[l], ffn_w_down[l])
    return x
```

```python
import functools

import jax
import jax.numpy as jnp
from jax import lax
from jax.experimental import pallas as pl
from jax.experimental.pallas import tpu as pltpu

F32 = jnp.float32
BF16 = jnp.bfloat16

EPS = 1e-6
NEG_INF = -1e30
HEAD_DIM = 128
HGRN_CHUNK = 64
CONV_W = 3
LOG2E = 1.4426950408889634
BIAS_LANES = 16

V7X_VMEM_BYTES = 64 * 1024 * 1024
VMEM_LIMIT_BYTES = V7X_VMEM_BYTES - 8 * 1024 * 1024

MOD_TN = 512
HGRN_TS = 256
FFN_TM = 512
FFN_FC = 256
KV_TS = 512
ATT_TQ = 256
ATT_TK = 256


def _dot(a, b):
    return jnp.dot(a, b, preferred_element_type=F32)


def _dot_nt(a, b):
    return lax.dot_general(a, b, (((1,), (1,)), ((), ())), preferred_element_type=F32)


def _dot_tn(a, b):
    return lax.dot_general(a, b, (((0,), (0,)), ((), ())), preferred_element_type=F32)


def _sigmoid(x):
    return 1.0 / (1.0 + jnp.exp(-x))


def _rms(x):
    return x * lax.rsqrt(jnp.mean(x * x, axis=-1, keepdims=True) + EPS)


def _modulate(x, shift, scale):
    return _rms(x) * (1.0 + scale) + shift


def _split_bf16(x):
    hi = x.astype(BF16)
    lo = (x - hi.astype(F32)).astype(BF16)
    return hi, lo


def _resident(block_shape, index_map):
    return pl.BlockSpec(block_shape, index_map, pipeline_mode=pl.Buffered(1))


def _params(semantics):
    return pltpu.CompilerParams(dimension_semantics=semantics, vmem_limit_bytes=VMEM_LIMIT_BYTES)


def _mod_kernel(c_ref, w_ref, b_ref, o_ref):
    c = c_ref[...]
    ca_hi, ca_lo = _split_bf16(c * _sigmoid(c))
    w_hi, w_lo = _split_bf16(w_ref[...])
    acc = _dot(ca_hi, w_hi) + _dot(ca_lo, w_hi) + _dot(ca_hi, w_lo)
    o_ref[...] = acc + b_ref[...]


def _mod_call(c, w3, b3, layer):
    bsz, d = c.shape
    n = w3.shape[-1]
    tn = MOD_TN
    return pl.pallas_call(
        _mod_kernel,
        grid=(n // tn,),
        in_specs=[
            pl.BlockSpec((bsz, d), lambda j: (0, 0)),
            pl.BlockSpec((None, d, tn), lambda j: (layer, 0, j)),
            pl.BlockSpec((None, 1, tn), lambda j: (layer, 0, j)),
        ],
        out_specs=pl.BlockSpec((bsz, tn), lambda j: (0, j)),
        out_shape=jax.ShapeDtypeStruct((bsz, n), F32),
        compiler_params=_params(("arbitrary",)),
        name="adaln_mod",
    )(c, w3, b3)


def _hgrn_kernel(x_ref, mod_ref, win_ref, wout_ref, lb_ref, ng_ref, tri_ref, o_ref,
                 proj_sc, y_sc, st_sc, *, ts, heads, layer):
    dk = HEAD_DIM
    w = heads * dk
    chunk = HGRN_CHUNK
    nch = ts // chunk

    @pl.when(pl.program_id(1) == 0)
    def _():
        st_sc[...] = jnp.zeros_like(st_sc)

    x = x_ref[0]
    mod = mod_ref[0]
    h = _modulate(x, mod[0:1], mod[1:2]).astype(BF16)
    proj_sc[...] = _dot(h, win_ref[...])

    lb_e = jnp.exp(lb_ref[...] - jnp.max(lb_ref[...], axis=0, keepdims=True))
    lb = jnp.sum(lb_e[0:layer + 1], axis=0, keepdims=True) / jnp.sum(lb_e, axis=0, keepdims=True)
    fg = lb + (1.0 - lb) * _sigmoid(proj_sc[:, w:2 * w])
    logf_hi, logf_lo = _split_bf16(jnp.log(fg))
    tri = tri_ref[...]
    bc = _dot(tri, logf_hi) + _dot(tri, logf_lo)

    def per_chunk_row(r):
        return jnp.concatenate(
            [jnp.broadcast_to(bc[c * chunk + r:c * chunk + r + 1], (chunk, w)) for c in range(nch)], axis=0)

    b_mid = per_chunk_row(chunk // 2)
    b_last = per_chunk_row(chunk - 1)

    q = proj_sc[:, 0:w]
    q = q * _sigmoid(q)
    kk = 1.0 - fg
    q_intra = (q * jnp.exp(bc - b_mid)).astype(BF16)
    k_intra = (kk * jnp.exp(b_mid - bc)).astype(BF16)
    q_inter = (q * jnp.exp(bc)).astype(BF16)
    k_state = (kk * jnp.exp(b_last - bc)).astype(BF16)
    v = proj_sc[:, 2 * w:3 * w].astype(BF16)
    g = proj_sc[:, 3 * w:4 * w]
    gate = g * _sigmoid(g) * ng_ref[...]

    row = lax.broadcasted_iota(jnp.int32, (ts, ts), 0)
    col = lax.broadcasted_iota(jnp.int32, (ts, ts), 1)
    mask = (row - col).astype(jnp.uint32) <= (row & (chunk - 1)).astype(jnp.uint32)

    for hd in range(heads):
        cs = slice(hd * dk, (hd + 1) * dk)
        s = _dot_nt(q_intra[:, cs], k_intra[:, cs])
        s = jnp.where(mask, s, 0.0).astype(BF16)
        o_intra = _dot(s, v[:, cs])
        st = st_sc[hd]
        outs = []
        for c in range(nch):
            rs = slice(c * chunk, (c + 1) * chunk)
            outs.append(o_intra[rs] + _dot_nt(q_inter[rs, cs], st.astype(BF16)))
            decay = jnp.exp(bc[(c + 1) * chunk - 1:(c + 1) * chunk, cs])
            st = decay * st + _dot_tn(v[rs, cs], k_state[rs, cs])
        st_sc[hd] = st
        o = jnp.concatenate(outs, axis=0)
        y_sc[:, cs] = (_rms(o) * gate[:, cs]).astype(BF16)

    o_ref[0] = x + mod[2:3] * _dot(y_sc[...], wout_ref[...])


def _hgrn_call(x, mod, w_in, w_out, lb_logits, layer, norm_g):
    bsz, seq, d = x.shape
    ts = HGRN_TS
    heads = w_out.shape[0] // HEAD_DIM
    w = heads * HEAD_DIM
    n_lb = lb_logits.shape[0]
    idx = jnp.arange(ts)
    tri = ((idx[:, None] >= idx[None, :]) & (idx[:, None] // HGRN_CHUNK == idx[None, :] // HGRN_CHUNK)).astype(BF16)
    kern = functools.partial(_hgrn_kernel, ts=ts, heads=heads, layer=layer)
    return pl.pallas_call(
        kern,
        grid=(bsz, seq // ts),
        in_specs=[
            pl.BlockSpec((1, ts, d), lambda b, s: (b, s, 0)),
            pl.BlockSpec((1, 6, d), lambda b, s: (b, 0, 0)),
            _resident((d, 4 * w), lambda b, s: (0, 0)),
            _resident((w, d), lambda b, s: (0, 0)),
            _resident((n_lb, w), lambda b, s: (0, 0)),
            _resident((1, w), lambda b, s: (0, 0)),
            _resident((ts, ts), lambda b, s: (0, 0)),
        ],
        out_specs=pl.BlockSpec((1, ts, d), lambda b, s: (b, s, 0)),
        out_shape=jax.ShapeDtypeStruct(x.shape, F32),
        scratch_shapes=[
            pltpu.VMEM((ts, 4 * w), F32),
            pltpu.VMEM((ts, w), BF16),
            pltpu.VMEM((heads, HEAD_DIM, HEAD_DIM), F32),
        ],
        compiler_params=_params(("arbitrary", "arbitrary")),
        name="hgrn2_layer",
    )(x, mod, w_in.astype(BF16), w_out.astype(BF16), lb_logits.astype(F32),
      jnp.tile(norm_g, heads).reshape(1, w), tri)


def _ffn_kernel(x_ref, mod_ref, wup_ref, cw_ref, cb_ref, wdn_ref, o_ref, tail_sc, *, tm, ff):
    fc = FFN_FC

    @pl.when(pl.program_id(1) == 0)
    def _():
        tail_sc[...] = jnp.zeros_like(tail_sc)

    x = x_ref[0]
    mod = mod_ref[0]
    h = _modulate(x, mod[3:4], mod[4:5]).astype(BF16)

    def conv_cols(c0):
        u = _dot(h, wup_ref[:, c0:c0 + fc])
        prev = tail_sc[:, c0:c0 + fc]
        tail_sc[:, c0:c0 + fc] = u[tm - 8:tm]
        ext = jnp.concatenate([prev, u], axis=0)
        cw = cw_ref[:, c0:c0 + fc]
        return (cw[2:3] * u + cw[1:2] * ext[7:tm + 7] + cw[0:1] * ext[6:tm + 6]
                + cb_ref[:, c0:c0 + fc])

    acc = jnp.zeros((tm, x.shape[-1]), F32)
    for j in range(ff // fc):
        gate = conv_cols(j * fc)
        val = conv_cols(ff + j * fc)
        act = (gate * _sigmoid(gate) * val).astype(BF16)
        acc = acc + _dot(act, wdn_ref[j * fc:(j + 1) * fc, :])
    o_ref[0] = x + mod[5:6] * acc


def _ffn_call(x, mod, w_up, conv_w, conv_b, w_down):
    bsz, seq, d = x.shape
    ff = w_down.shape[0]
    tm = FFN_TM
    kern = functools.partial(_ffn_kernel, tm=tm, ff=ff)
    return pl.pallas_call(
        kern,
        grid=(bsz, seq // tm),
        in_specs=[
            pl.BlockSpec((1, tm, d), lambda b, s: (b, s, 0)),
            pl.BlockSpec((1, 6, d), lambda b, s: (b, 0, 0)),
            _resident((d, 2 * ff), lambda b, s: (0, 0)),
            _resident((CONV_W, 2 * ff), lambda b, s: (0, 0)),
            _resident((1, 2 * ff), lambda b, s: (0, 0)),
            _resident((ff, d), lambda b, s: (0, 0)),
        ],
        out_specs=pl.BlockSpec((1, tm, d), lambda b, s: (b, s, 0)),
        out_shape=jax.ShapeDtypeStruct(x.shape, F32),
        scratch_shapes=[pltpu.VMEM((8, 2 * ff), F32)],
        compiler_params=_params(("arbitrary", "arbitrary")),
        name="conv_glu_ffn",
    )(x, mod, w_up.astype(BF16), conv_w, conv_b.reshape(1, 2 * ff), w_down.astype(BF16))


def _kv_kernel(x_ref, mod_ref, wkv_ref, wf_ref, bf_ref, kg_ref, tri_ref, k_ref, v_ref, f_ref, carry_sc,
               *, ts, heads):
    dh = HEAD_DIM
    w = heads * dh

    @pl.when(pl.program_id(1) == 0)
    def _():
        carry_sc[...] = jnp.zeros_like(carry_sc)

    mod = mod_ref[0]
    h = _modulate(x_ref[0], mod[0:1], mod[1:2]).astype(BF16)
    kv = _dot(h, wkv_ref[...])
    kg = kg_ref[...]
    for hd in range(heads):
        cs = slice(hd * dh, (hd + 1) * dh)
        k_ref[0, :, cs] = (_rms(kv[:, cs]) * kg).astype(BF16)
    v_ref[0] = kv[:, w:2 * w].astype(BF16)

    z = _dot(h, wf_ref[...]) + bf_ref[...]
    log_f = jnp.minimum(z, 0.0) - jnp.log(1.0 + jnp.exp(-jnp.abs(z)))
    hi, lo = _split_bf16(log_f)
    tri = tri_ref[...]
    cum = _dot(tri, hi) + _dot(tri, lo) + carry_sc[0:1, :]
    carry_sc[...] = jnp.broadcast_to(cum[ts - 1:ts, :], carry_sc.shape)
    t0 = -LOG2E * cum
    b0 = t0.astype(BF16).astype(F32)
    t1 = t0 - b0
    b1 = t1.astype(BF16).astype(F32)
    b2 = t1 - b1
    sub = lax.broadcasted_iota(jnp.int32, (ts, dh), 1) & (BIAS_LANES - 1)
    f_ref[0] = jnp.where(sub == 0, b0, jnp.where(sub == 1, b1, jnp.where(sub == 2, b2, 0.0))).astype(BF16)


def _kv_call(x, mod, kv_w, kv_b_f, k_norm_g):
    bsz, seq, d = x.shape
    heads = kv_b_f.shape[0]
    w = heads * HEAD_DIM
    ts = KV_TS
    assert heads * BIAS_LANES <= HEAD_DIM
    idx = jnp.arange(ts)
    tri = (idx[:, None] >= idx[None, :]).astype(BF16)
    lane = jnp.arange(HEAD_DIM)
    head_of_lane = jnp.minimum(lane // BIAS_LANES, heads - 1)
    used = (lane // BIAS_LANES < heads) & (lane % BIAS_LANES < 3)
    wf = jnp.where(used[None, :], kv_w[:, 2 * w:][:, head_of_lane], 0.0).astype(BF16)
    bf = jnp.where(used, kv_b_f.astype(F32)[head_of_lane], 0.0).reshape(1, HEAD_DIM)
    kern = functools.partial(_kv_kernel, ts=ts, heads=heads)
    return pl.pallas_call(
        kern,
        grid=(bsz, seq // ts),
        in_specs=[
            pl.BlockSpec((1, ts, d), lambda b, s: (b, s, 0)),
            pl.BlockSpec((1, 2, d), lambda b, s: (b, 0, 0)),
            _resident((d, 2 * w), lambda b, s: (0, 0)),
            _resident((d, HEAD_DIM), lambda b, s: (0, 0)),
            _resident((1, HEAD_DIM), lambda b, s: (0, 0)),
            _resident((1, HEAD_DIM), lambda b, s: (0, 0)),
            _resident((ts, ts), lambda b, s: (0, 0)),
        ],
        out_specs=[
            pl.BlockSpec((1, ts, w), lambda b, s: (b, s, 0)),
            pl.BlockSpec((1, ts, w), lambda b, s: (b, s, 0)),
            pl.BlockSpec((1, ts, HEAD_DIM), lambda b, s: (b, s, 0)),
        ],
        out_shape=[
            jax.ShapeDtypeStruct((bsz, seq, w), BF16),
            jax.ShapeDtypeStruct((bsz, seq, w), BF16),
            jax.ShapeDtypeStruct((bsz, seq, HEAD_DIM), BF16),
        ],
        scratch_shapes=[pltpu.VMEM((8, HEAD_DIM), F32)],
        compiler_params=_params(("arbitrary", "arbitrary")),
        name="shared_kv",
    )(x, mod, kv_w[:, :2 * w].astype(BF16), wf, bf, k_norm_g.reshape(1, HEAD_DIM), tri)


def _fox_kernel(x_ref, mod_ref, wq_ref, qg_ref, k_ref, v_ref, f_ref, wout_ref, o_ref,
                q_sc, gate_sc, y_sc, m_sc, acc_sc, *, tq, tk, heads):
    dh = HEAD_DIM
    w = heads * dh
    sub_tiles = tq // tk
    qi = pl.program_id(1)

    x = x_ref[0]
    mod = mod_ref[0]
    h = _modulate(x, mod[0:1], mod[1:2]).astype(BF16)
    pq = _dot(h, wq_ref[...])
    gate_sc[...] = _sigmoid(pq[:, w:2 * w])
    qg = qg_ref[...] * (dh ** -0.5 * LOG2E)
    lane = lax.broadcasted_iota(jnp.int32, (tq, dh), 1)
    for hd in range(heads):
        cs = slice(hd * dh, (hd + 1) * dh)
        q_sc[hd, :, 0:dh] = (_rms(pq[:, cs]) * qg).astype(BF16)
        picks_bias = (lane - hd * BIAS_LANES).astype(jnp.uint32) < 3
        q_sc[hd, :, dh:2 * dh] = jnp.where(picks_bias, 1.0, 0.0).astype(BF16)
    m_sc[...] = jnp.full(m_sc.shape, NEG_INF, F32)
    acc_sc[...] = jnp.zeros(acc_sc.shape, F32)

    ones = jnp.ones((tk, dh), BF16)
    row = lax.broadcasted_iota(jnp.int32, (tq, tk), 0)
    col = lax.broadcasted_iota(jnp.int32, (tq, tk), 1)

    def step(j, diag_sub):
        ks = pl.multiple_of(j * tk, tk)
        f_tile = f_ref[0, pl.ds(ks, tk), :]
        for hd in range(heads):
            cs = slice(hd * dh, (hd + 1) * dh)
            k_aug = jnp.concatenate([k_ref[0, pl.ds(ks, tk), cs], f_tile], axis=1)
            v_aug = jnp.concatenate([v_ref[0, pl.ds(ks, tk), cs], ones], axis=1)
            s = _dot_nt(q_sc[hd], k_aug)
            if diag_sub is not None:
                s = jnp.where(col + diag_sub * tk <= row, s, NEG_INF)
            m_prev = m_sc[hd]
            m_new = jnp.maximum(m_prev, jnp.broadcast_to(jnp.max(s, axis=-1, keepdims=True), (tq, dh)))
            alpha = jnp.exp2(m_prev - m_new)
            p = jnp.exp2(s - jnp.concatenate([m_new] * (tk // dh), axis=1)).astype(BF16)
            acc_sc[hd] = jnp.concatenate([alpha, alpha], axis=1) * acc_sc[hd] + _dot(p, v_aug)
            m_sc[hd] = m_new

    def body(j, carry):
        step(j, None)
        return carry

    lax.fori_loop(0, qi * sub_tiles, body, 0)
    for d in range(sub_tiles):
        step(qi * sub_tiles + d, d)

    for hd in range(heads):
        cs = slice(hd * dh, (hd + 1) * dh)
        acc = acc_sc[hd]
        y_sc[:, cs] = (acc[:, 0:dh] / acc[:, dh:2 * dh] * gate_sc[:, cs]).astype(BF16)

    o_ref[0] = x + mod[2:3] * _dot(y_sc[...], wout_ref[...])


def _fox_call(x, mod, w_q, q_norm_g, w_out, k, v, fbias):
    bsz, seq, d = x.shape
    w = w_out.shape[0]
    heads = w // HEAD_DIM
    tq, tk = ATT_TQ, ATT_TK
    kern = functools.partial(_fox_kernel, tq=tq, tk=tk, heads=heads)
    return pl.pallas_call(
        kern,
        grid=(bsz, seq // tq),
        in_specs=[
            pl.BlockSpec((1, tq, d), lambda b, s: (b, s, 0)),
            pl.BlockSpec((1, 6, d), lambda b, s: (b, 0, 0)),
            _resident((d, 2 * w), lambda b, s: (0, 0)),
            _resident((1, HEAD_DIM), lambda b, s: (0, 0)),
            _resident((1, seq, w), lambda b, s: (b, 0, 0)),
            _resident((1, seq, w), lambda b, s: (b, 0, 0)),
            _resident((1, seq, HEAD_DIM), lambda b, s: (b, 0, 0)),
            _resident((w, d), lambda b, s: (0, 0)),
        ],
        out_specs=pl.BlockSpec((1, tq, d), lambda b, s: (b, s, 0)),
        out_shape=jax.ShapeDtypeStruct(x.shape, F32),
        scratch_shapes=[
            pltpu.VMEM((heads, tq, 2 * HEAD_DIM), BF16),
            pltpu.VMEM((tq, w), F32),
            pltpu.VMEM((tq, w), BF16),
            pltpu.VMEM((heads, tq, HEAD_DIM), F32),
            pltpu.VMEM((heads, tq, 2 * HEAD_DIM), F32),
        ],
        compiler_params=_params(("arbitrary", "arbitrary")),
        name="fox_layer",
    )(x, mod, w_q.astype(BF16), q_norm_g.reshape(1, HEAD_DIM), k, v, fbias, w_out.astype(BF16))


def kernel(x, c, ada_w, ada_b, a_w_in, a_lb_logits, a_norm_g, a_w_out, kv_ada_w, kv_ada_b, kv_w, kv_b_f, k_norm_g,
           b_w_q, q_norm_g, b_w_out, ffn_w_up, ffn_conv_w, ffn_conv_b, ffn_w_down):
    bsz, seq, d = x.shape
    depth = ada_w.shape[0]
    n_a = a_w_in.shape[0]
    ada_b3 = ada_b.reshape(depth, 1, 6 * d)
    k_sh = v_sh = fbias = None
    for l in range(depth):
        mod = _mod_call(c, ada_w, ada_b3, l).reshape(bsz, 6, d)
        if l == n_a:
            kv_mod = _mod_call(c, kv_ada_w[None], kv_ada_b.reshape(1, 1, 2 * d), 0).reshape(bsz, 2, d)
            k_sh, v_sh, fbias = _kv_call(x, kv_mod, kv_w, kv_b_f, k_norm_g)
        if l < n_a:
            x = _hgrn_call(x, mod, a_w_in[l], a_w_out[l], a_lb_logits, l, a_norm_g[l])
        else:
            j = l - n_a
            x = _fox_call(x, mod, b_w_q[j], q_norm_g[j], b_w_out[j], k_sh, v_sh, fbias)
        x = _ffn_call(x, mod, ffn_w_up[l], ffn_conv_w[l], ffn_conv_b[l], ffn_w_down[l])
    return x
```

```python
import functools

import jax
import jax.numpy as jnp
from jax import lax
from jax.experimental import pallas as pl
from jax.experimental.pallas import tpu as pltpu

F32 = jnp.float32
BF16 = jnp.bfloat16

EPS = 1e-6
NEG_INF = -1e30
HEAD_DIM = 128
HGRN_CHUNK = 64
CONV_W = 3
LOG2E = 1.4426950408889634
BIAS_LANES = 16

V7X_VMEM_BYTES = 64 * 1024 * 1024
VMEM_LIMIT_BYTES = V7X_VMEM_BYTES - 8 * 1024 * 1024

MOD_TN = 512
HGRN_TS = 256
FFN_TM = 512
FFN_FC = 256
KV_TS = 512
ATT_TQ = 256
ATT_TK = 512


def _dot(a, b):
    return jnp.dot(a, b, preferred_element_type=F32)


def _dot_nt(a, b):
    return lax.dot_general(a, b, (((1,), (1,)), ((), ())), preferred_element_type=F32)


def _dot_tn(a, b):
    return lax.dot_general(a, b, (((0,), (0,)), ((), ())), preferred_element_type=F32)


def _sigmoid(x):
    return 1.0 / (1.0 + jnp.exp(-x))


def _rms(x):
    return x * lax.rsqrt(jnp.mean(x * x, axis=-1, keepdims=True) + EPS)


def _modulate(x, shift, scale):
    return _rms(x) * (1.0 + scale) + shift


def _split_bf16(x):
    hi = x.astype(BF16)
    lo = (x - hi.astype(F32)).astype(BF16)
    return hi, lo


def _resident(block_shape, index_map):
    return pl.BlockSpec(block_shape, index_map, pipeline_mode=pl.Buffered(1))


def _params(semantics):
    return pltpu.CompilerParams(dimension_semantics=semantics, vmem_limit_bytes=VMEM_LIMIT_BYTES)


def _mod_kernel(c_ref, w_ref, b_ref, o_ref):
    c = c_ref[...]
    ca_hi, ca_lo = _split_bf16(c * _sigmoid(c))
    w_hi, w_lo = _split_bf16(w_ref[...])
    acc = _dot(ca_hi, w_hi) + _dot(ca_lo, w_hi) + _dot(ca_hi, w_lo)
    o_ref[...] = acc + b_ref[...]


def _mod_call(c, w3, b3, layer):
    bsz, d = c.shape
    n = w3.shape[-1]
    tn = MOD_TN
    return pl.pallas_call(
        _mod_kernel,
        grid=(n // tn,),
        in_specs=[
            pl.BlockSpec((bsz, d), lambda j: (0, 0)),
            pl.BlockSpec((None, d, tn), lambda j: (layer, 0, j)),
            pl.BlockSpec((None, 1, tn), lambda j: (layer, 0, j)),
        ],
        out_specs=pl.BlockSpec((bsz, tn), lambda j: (0, j)),
        out_shape=jax.ShapeDtypeStruct((bsz, n), F32),
        compiler_params=_params(("arbitrary",)),
        name="adaln_mod",
    )(c, w3, b3)


def _hgrn_kernel(x_ref, mod_ref, win_ref, wout_ref, lb_ref, ng_ref, tri_ref, o_ref,
                 proj_sc, y_sc, st_sc, *, ts, heads, layer):
    dk = HEAD_DIM
    w = heads * dk
    chunk = HGRN_CHUNK
    nch = ts // chunk

    @pl.when(pl.program_id(1) == 0)
    def _():
        st_sc[...] = jnp.zeros_like(st_sc)

    x = x_ref[0]
    mod = mod_ref[0]
    h = _modulate(x, mod[0:1], mod[1:2]).astype(BF16)
    proj_sc[...] = _dot(h, win_ref[...])

    lb_e = jnp.exp(lb_ref[...] - jnp.max(lb_ref[...], axis=0, keepdims=True))
    lb = jnp.sum(lb_e[0:layer + 1], axis=0, keepdims=True) / jnp.sum(lb_e, axis=0, keepdims=True)
    fg = lb + (1.0 - lb) * _sigmoid(proj_sc[:, w:2 * w])
    logf_hi, logf_lo = _split_bf16(jnp.log(fg))
    tri = tri_ref[...]
    bc = _dot(tri, logf_hi) + _dot(tri, logf_lo)

    def per_chunk_row(r):
        return jnp.concatenate(
            [jnp.broadcast_to(bc[c * chunk + r:c * chunk + r + 1], (chunk, w)) for c in range(nch)], axis=0)

    b_mid = per_chunk_row(chunk // 2)
    b_last = per_chunk_row(chunk - 1)

    q = proj_sc[:, 0:w]
    q = q * _sigmoid(q)
    kk = 1.0 - fg
    q_intra = (q * jnp.exp(bc - b_mid)).astype(BF16)
    k_intra = (kk * jnp.exp(b_mid - bc)).astype(BF16)
    q_inter = (q * jnp.exp(bc)).astype(BF16)
    k_state = (kk * jnp.exp(b_last - bc)).astype(BF16)
    v = proj_sc[:, 2 * w:3 * w].astype(BF16)
    g = proj_sc[:, 3 * w:4 * w]
    gate = g * _sigmoid(g) * ng_ref[...]

    row = lax.broadcasted_iota(jnp.int32, (ts, ts), 0)
    col = lax.broadcasted_iota(jnp.int32, (ts, ts), 1)
    mask = (row - col).astype(jnp.uint32) <= (row & (chunk - 1)).astype(jnp.uint32)

    for hd in range(heads):
        cs = slice(hd * dk, (hd + 1) * dk)
        s = _dot_nt(q_intra[:, cs], k_intra[:, cs])
        s = jnp.where(mask, s, 0.0).astype(BF16)
        o_intra = _dot(s, v[:, cs])
        st = st_sc[hd]
        outs = []
        for c in range(nch):
            rs = slice(c * chunk, (c + 1) * chunk)
            outs.append(o_intra[rs] + _dot_nt(q_inter[rs, cs], st.astype(BF16)))
            decay = jnp.exp(bc[(c + 1) * chunk - 1:(c + 1) * chunk, cs])
            st = decay * st + _dot_tn(v[rs, cs], k_state[rs, cs])
        st_sc[hd] = st
        o = jnp.concatenate(outs, axis=0)
        y_sc[:, cs] = (_rms(o) * gate[:, cs]).astype(BF16)

    o_ref[0] = x + mod[2:3] * _dot(y_sc[...], wout_ref[...])


def _hgrn_call(x, mod, w_in, w_out, lb_logits, layer, norm_g):
    bsz, seq, d = x.shape
    ts = HGRN_TS
    heads = w_out.shape[0] // HEAD_DIM
    w = heads * HEAD_DIM
    n_lb = lb_logits.shape[0]
    idx = jnp.arange(ts)
    tri = ((idx[:, None] >= idx[None, :]) & (idx[:, None] // HGRN_CHUNK == idx[None, :] // HGRN_CHUNK)).astype(BF16)
    kern = functools.partial(_hgrn_kernel, ts=ts, heads=heads, layer=layer)
    return pl.pallas_call(
        kern,
        grid=(bsz, seq // ts),
        in_specs=[
            pl.BlockSpec((1, ts, d), lambda b, s: (b, s, 0)),
            pl.BlockSpec((1, 6, d), lambda b, s: (b, 0, 0)),
            _resident((d, 4 * w), lambda b, s: (0, 0)),
            _resident((w, d), lambda b, s: (0, 0)),
            _resident((n_lb, w), lambda b, s: (0, 0)),
            _resident((1, w), lambda b, s: (0, 0)),
            _resident((ts, ts), lambda b, s: (0, 0)),
        ],
        out_specs=pl.BlockSpec((1, ts, d), lambda b, s: (b, s, 0)),
        out_shape=jax.ShapeDtypeStruct(x.shape, F32),
        scratch_shapes=[
            pltpu.VMEM((ts, 4 * w), F32),
            pltpu.VMEM((ts, w), BF16),
            pltpu.VMEM((heads, HEAD_DIM, HEAD_DIM), F32),
        ],
        compiler_params=_params(("arbitrary", "arbitrary")),
        name="hgrn2_layer",
    )(x, mod, w_in.astype(BF16), w_out.astype(BF16), lb_logits.astype(F32),
      jnp.tile(norm_g, heads).reshape(1, w), tri)


def _ffn_kernel(x_ref, mod_ref, wup_ref, cw_ref, cb_ref, wdn_ref, o_ref, tail_sc, perm_sc, act_sc, *, tm, ff):
    fc = FFN_FC
    d = x_ref.shape[-1]
    nj = tm // 8
    n_slab = d // 128

    @pl.when(pl.program_id(1) == 0)
    def _():
        tail_sc[...] = jnp.zeros_like(tail_sc)

    def perm_rows(jn):
        g = nj // 8
        return pl.ds((jn % g) * 64 + jn // g, 8, stride=8)

    x = x_ref[0]
    mod = mod_ref[0]
    h_nat = _modulate(x, mod[3:4], mod[4:5])
    for sl in range(n_slab):
        for jn in range(nj):
            perm_sc[sl, perm_rows(jn), :] = h_nat[8 * jn:8 * jn + 8, sl * 128:(sl + 1) * 128]
    h = jnp.concatenate([perm_sc[sl] for sl in range(n_slab)], axis=1).astype(BF16)

    first_row = lax.broadcasted_iota(jnp.int32, (8, fc), 0) == 0

    def conv_cols(c0):
        u = _dot(h, wup_ref[:, c0:c0 + fc])
        prev = tail_sc[:, c0:c0 + fc]
        tail_sc[:, c0:c0 + fc] = u[tm - 16:tm]
        back1 = jnp.where(first_row, pltpu.roll(prev[8:16], 1, 0), pltpu.roll(u[tm - 8:tm], 1, 0))
        back2 = jnp.where(first_row, pltpu.roll(prev[0:8], 1, 0), pltpu.roll(u[tm - 16:tm - 8], 1, 0))
        u1 = jnp.concatenate([back1, u[0:tm - 8]], axis=0)
        u2 = jnp.concatenate([back2, back1, u[0:tm - 16]], axis=0)
        cw = cw_ref[:, c0:c0 + fc]
        return cw[2:3] * u + cw[1:2] * u1 + cw[0:1] * u2 + cb_ref[:, c0:c0 + fc]

    for j in range(ff // fc):
        gate = conv_cols(j * fc)
        val = conv_cols(ff + j * fc)
        act_sc[:, j * fc:(j + 1) * fc] = (gate * _sigmoid(gate) * val).astype(BF16)
    y = _dot(act_sc[...], wdn_ref[...])

    for sl in range(n_slab):
        perm_sc[sl] = y[:, sl * 128:(sl + 1) * 128]
    y_nat = jnp.concatenate(
        [jnp.concatenate([perm_sc[sl, perm_rows(jn), :] for jn in range(nj)], axis=0) for sl in range(n_slab)],
        axis=1)
    o_ref[0] = x + mod[5:6] * y_nat


def _ffn_call(x, mod, w_up, conv_w, conv_b, w_down):
    bsz, seq, d = x.shape
    ff = w_down.shape[0]
    tm = FFN_TM
    kern = functools.partial(_ffn_kernel, tm=tm, ff=ff)
    return pl.pallas_call(
        kern,
        grid=(bsz, seq // tm),
        in_specs=[
            pl.BlockSpec((1, tm, d), lambda b, s: (b, s, 0)),
            pl.BlockSpec((1, 6, d), lambda b, s: (b, 0, 0)),
            _resident((d, 2 * ff), lambda b, s: (0, 0)),
            _resident((CONV_W, 2 * ff), lambda b, s: (0, 0)),
            _resident((1, 2 * ff), lambda b, s: (0, 0)),
            _resident((ff, d), lambda b, s: (0, 0)),
        ],
        out_specs=pl.BlockSpec((1, tm, d), lambda b, s: (b, s, 0)),
        out_shape=jax.ShapeDtypeStruct(x.shape, F32),
        scratch_shapes=[
            pltpu.VMEM((16, 2 * ff), F32),
            pltpu.VMEM((d // 128, tm, 128), F32),
            pltpu.VMEM((tm, ff), BF16),
        ],
        compiler_params=_params(("arbitrary", "arbitrary")),
        name="conv_glu_ffn",
    )(x, mod, w_up.astype(BF16), conv_w, conv_b.reshape(1, 2 * ff), w_down.astype(BF16))


def _kv_kernel(x_ref, mod_ref, wkv_ref, wf_ref, bf_ref, kg_ref, tri_ref, k_ref, v_ref, f_ref, carry_sc,
               *, ts, heads):
    dh = HEAD_DIM
    w = heads * dh

    @pl.when(pl.program_id(1) == 0)
    def _():
        carry_sc[...] = jnp.zeros_like(carry_sc)

    mod = mod_ref[0]
    h = _modulate(x_ref[0], mod[0:1], mod[1:2]).astype(BF16)
    kv = _dot(h, wkv_ref[...])
    kg = kg_ref[...]
    for hd in range(heads):
        cs = slice(hd * dh, (hd + 1) * dh)
        k_ref[0, :, cs] = (_rms(kv[:, cs]) * kg).astype(BF16)
    v_ref[0] = kv[:, w:2 * w].astype(BF16)

    z = _dot(h, wf_ref[...]) + bf_ref[...]
    log_f = jnp.minimum(z, 0.0) - jnp.log(1.0 + jnp.exp(-jnp.abs(z)))
    hi, lo = _split_bf16(log_f)
    tri = tri_ref[...]
    cum = _dot(tri, hi) + _dot(tri, lo) + carry_sc[0:1, :]
    carry_sc[...] = jnp.broadcast_to(cum[ts - 1:ts, :], carry_sc.shape)
    t0 = -LOG2E * cum
    b0 = t0.astype(BF16).astype(F32)
    t1 = t0 - b0
    b1 = t1.astype(BF16).astype(F32)
    b2 = t1 - b1
    sub = lax.broadcasted_iota(jnp.int32, (ts, dh), 1) & (BIAS_LANES - 1)
    f_ref[0] = jnp.where(sub == 0, b0, jnp.where(sub == 1, b1, jnp.where(sub == 2, b2, 0.0))).astype(BF16)


def _kv_call(x, mod, kv_w, kv_b_f, k_norm_g):
    bsz, seq, d = x.shape
    heads = kv_b_f.shape[0]
    w = heads * HEAD_DIM
    ts = KV_TS
    assert heads * BIAS_LANES <= HEAD_DIM
    idx = jnp.arange(ts)
    tri = (idx[:, None] >= idx[None, :]).astype(BF16)
    lane = jnp.arange(HEAD_DIM)
    head_of_lane = jnp.minimum(lane // BIAS_LANES, heads - 1)
    used = (lane // BIAS_LANES < heads) & (lane % BIAS_LANES < 3)
    wf = jnp.where(used[None, :], kv_w[:, 2 * w:][:, head_of_lane], 0.0).astype(BF16)
    bf = jnp.where(used, kv_b_f.astype(F32)[head_of_lane], 0.0).reshape(1, HEAD_DIM)
    kern = functools.partial(_kv_kernel, ts=ts, heads=heads)
    return pl.pallas_call(
        kern,
        grid=(bsz, seq // ts),
        in_specs=[
            pl.BlockSpec((1, ts, d), lambda b, s: (b, s, 0)),
            pl.BlockSpec((1, 2, d), lambda b, s: (b, 0, 0)),
            _resident((d, 2 * w), lambda b, s: (0, 0)),
            _resident((d, HEAD_DIM), lambda b, s: (0, 0)),
            _resident((1, HEAD_DIM), lambda b, s: (0, 0)),
            _resident((1, HEAD_DIM), lambda b, s: (0, 0)),
            _resident((ts, ts), lambda b, s: (0, 0)),
        ],
        out_specs=[
            pl.BlockSpec((1, ts, w), lambda b, s: (b, s, 0)),
            pl.BlockSpec((1, ts, w), lambda b, s: (b, s, 0)),
            pl.BlockSpec((1, ts, HEAD_DIM), lambda b, s: (b, s, 0)),
        ],
        out_shape=[
            jax.ShapeDtypeStruct((bsz, seq, w), BF16),
            jax.ShapeDtypeStruct((bsz, seq, w), BF16),
            jax.ShapeDtypeStruct((bsz, seq, HEAD_DIM), BF16),
        ],
        scratch_shapes=[pltpu.VMEM((8, HEAD_DIM), F32)],
        compiler_params=_params(("arbitrary", "arbitrary")),
        name="shared_kv",
    )(x, mod, kv_w[:, :2 * w].astype(BF16), wf, bf, k_norm_g.reshape(1, HEAD_DIM), tri)


def _fox_kernel(x_ref, mod_ref, wq_ref, qg_ref, k_ref, v_ref, f_ref, wout_ref, o_ref,
                q_sc, gate_sc, y_sc, m_sc, acc_sc, *, tq, tk, heads):
    dh = HEAD_DIM
    w = heads * dh
    qi = pl.program_id(1)

    x = x_ref[0]
    mod = mod_ref[0]
    h = _modulate(x, mod[0:1], mod[1:2]).astype(BF16)
    pq = _dot(h, wq_ref[...])
    gate_sc[...] = _sigmoid(pq[:, w:2 * w])
    qg = qg_ref[...] * (dh ** -0.5 * LOG2E)
    lane = lax.broadcasted_iota(jnp.int32, (tq, dh), 1)
    for hd in range(heads):
        cs = slice(hd * dh, (hd + 1) * dh)
        q_sc[hd, :, 0:dh] = (_rms(pq[:, cs]) * qg).astype(BF16)
        picks_bias = (lane - hd * BIAS_LANES).astype(jnp.uint32) < 3
        q_sc[hd, :, dh:2 * dh] = jnp.where(picks_bias, 1.0, 0.0).astype(BF16)
    m_sc[...] = jnp.full(m_sc.shape, NEG_INF, F32)
    acc_sc[...] = jnp.zeros(acc_sc.shape, F32)

    def step(ks, width, masked):
        f_tile = f_ref[0, pl.ds(ks, width), :]
        ones = jnp.ones((width, dh), BF16)
        for hd in range(heads):
            cs = slice(hd * dh, (hd + 1) * dh)
            k_aug = jnp.concatenate([k_ref[0, pl.ds(ks, width), cs], f_tile], axis=1)
            v_aug = jnp.concatenate([v_ref[0, pl.ds(ks, width), cs], ones], axis=1)
            s = _dot_nt(q_sc[hd], k_aug)
            if masked:
                row = lax.broadcasted_iota(jnp.int32, (tq, width), 0)
                col = lax.broadcasted_iota(jnp.int32, (tq, width), 1)
                s = jnp.where(col <= row, s, NEG_INF)
            m_prev = m_sc[hd]
            m_new = jnp.maximum(m_prev, jnp.broadcast_to(jnp.max(s, axis=-1, keepdims=True), (tq, dh)))
            alpha = jnp.exp2(m_prev - m_new)
            p = jnp.exp2(s - jnp.concatenate([m_new] * (width // dh), axis=1)).astype(BF16)
            acc_sc[hd] = jnp.concatenate([alpha, alpha], axis=1) * acc_sc[hd] + _dot(p, v_aug)
            m_sc[hd] = m_new

    q0 = qi * tq
    n_wide = q0 // tk

    def body(j, carry):
        step(pl.multiple_of(j * tk, tk), tk, False)
        return carry

    lax.fori_loop(0, n_wide, body, 0)
    for r in range(tk // tq - 1):
        @pl.when(n_wide * tk + r * tq < q0)
        def _():
            step(pl.multiple_of(n_wide * tk + r * tq, tq), tq, False)
    step(pl.multiple_of(q0, tq), tq, True)

    for hd in range(heads):
        cs = slice(hd * dh, (hd + 1) * dh)
        acc = acc_sc[hd]
        y_sc[:, cs] = (acc[:, 0:dh] / acc[:, dh:2 * dh] * gate_sc[:, cs]).astype(BF16)

    o_ref[0] = x + mod[2:3] * _dot(y_sc[...], wout_ref[...])


def _fox_call(x, mod, w_q, q_norm_g, w_out, k, v, fbias):
    bsz, seq, d = x.shape
    w = w_out.shape[0]
    heads = w // HEAD_DIM
    tq, tk = ATT_TQ, ATT_TK
    kern = functools.partial(_fox_kernel, tq=tq, tk=tk, heads=heads)
    return pl.pallas_call(
        kern,
        grid=(bsz, seq // tq),
        in_specs=[
            pl.BlockSpec((1, tq, d), lambda b, s: (b, s, 0)),
            pl.BlockSpec((1, 6, d), lambda b, s: (b, 0, 0)),
            _resident((d, 2 * w), lambda b, s: (0, 0)),
            _resident((1, HEAD_DIM), lambda b, s: (0, 0)),
            _resident((1, seq, w), lambda b, s: (b, 0, 0)),
            _resident((1, seq, w), lambda b, s: (b, 0, 0)),
            _resident((1, seq, HEAD_DIM), lambda b, s: (b, 0, 0)),
            _resident((w, d), lambda b, s: (0, 0)),
        ],
        out_specs=pl.BlockSpec((1, tq, d), lambda b, s: (b, s, 0)),
        out_shape=jax.ShapeDtypeStruct(x.shape, F32),
        scratch_shapes=[
            pltpu.VMEM((heads, tq, 2 * HEAD_DIM), BF16),
            pltpu.VMEM((tq, w), F32),
            pltpu.VMEM((tq, w), BF16),
            pltpu.VMEM((heads, tq, HEAD_DIM), F32),
            pltpu.VMEM((heads, tq, 2 * HEAD_DIM), F32),
        ],
        compiler_params=_params(("arbitrary", "arbitrary")),
        name="fox_layer",
    )(x, mod, w_q.astype(BF16), q_norm_g.reshape(1, HEAD_DIM), k, v, fbias, w_out.astype(BF16))


def kernel(x, c, ada_w, ada_b, a_w_in, a_lb_logits, a_norm_g, a_w_out, kv_ada_w, kv_ada_b, kv_w, kv_b_f, k_norm_g,
           b_w_q, q_norm_g, b_w_out, ffn_w_up, ffn_conv_w, ffn_conv_b, ffn_w_down):
    bsz, seq, d = x.shape
    depth = ada_w.shape[0]
    n_a = a_w_in.shape[0]
    ada_b3 = ada_b.reshape(depth, 1, 6 * d)
    k_sh = v_sh = fbias = None
    for l in range(depth):
        mod = _mod_call(c, ada_w, ada_b3, l).reshape(bsz, 6, d)
        if l == n_a:
            kv_mod = _mod_call(c, kv_ada_w[None], kv_ada_b.reshape(1, 1, 2 * d), 0).reshape(bsz, 2, d)
            k_sh, v_sh, fbias = _kv_call(x, kv_mod, kv_w, kv_b_f, k_norm_g)
        if l < n_a:
            x = _hgrn_call(x, mod, a_w_in[l], a_w_out[l], a_lb_logits, l, a_norm_g[l])
        else:
            j = l - n_a
            x = _fox_call(x, mod, b_w_q[j], q_norm_g[j], b_w_out[j], k_sh, v_sh, fbias)
        x = _ffn_call(x, mod, ffn_w_up[l], ffn_conv_w[l], ffn_conv_b[l], ffn_w_down[l])
    return x
```

```python
import functools

import jax
import jax.numpy as jnp
from jax import lax
from jax.experimental import pallas as pl
from jax.experimental.pallas import tpu as pltpu

F32 = jnp.float32
BF16 = jnp.bfloat16

EPS = 1e-6
NEG_INF = -1e30
HEAD_DIM = 128
HGRN_CHUNK = 64
CONV_W = 3
LOG2E = 1.4426950408889634
BIAS_LANES = 16

V7X_VMEM_BYTES = 64 * 1024 * 1024
VMEM_LIMIT_BYTES = V7X_VMEM_BYTES - 8 * 1024 * 1024

MOD_TN = 1024
HGRN_TS = 512
HGRN_BLK = 256
FFN_TM = 512
FFN_FC = 256
KV_TS = 512
ATT_TQ = 512
ATT_TK = 512


def _dot(a, b):
    return jnp.dot(a, b, preferred_element_type=F32)


def _dot_nt(a, b):
    return lax.dot_general(a, b, (((1,), (1,)), ((), ())), preferred_element_type=F32)


def _dot_tn(a, b):
    return lax.dot_general(a, b, (((0,), (0,)), ((), ())), preferred_element_type=F32)


def _sigmoid(x):
    return 1.0 / (1.0 + jnp.exp(-x))


def _rms(x):
    return x * lax.rsqrt(jnp.mean(x * x, axis=-1, keepdims=True) + EPS)


def _modulate(x, shift, scale):
    return _rms(x) * (1.0 + scale) + shift


def _split_bf16(x):
    hi = x.astype(BF16)
    lo = (x - hi.astype(F32)).astype(BF16)
    return hi, lo


def _resident(block_shape, index_map):
    return pl.BlockSpec(block_shape, index_map, pipeline_mode=pl.Buffered(1))


def _params(semantics):
    return pltpu.CompilerParams(dimension_semantics=semantics, vmem_limit_bytes=VMEM_LIMIT_BYTES)


def _mod_kernel(c_ref, w_ref, b_ref, o_ref):
    c = c_ref[...]
    ca_hi, ca_lo = _split_bf16(c * _sigmoid(c))
    w_hi, w_lo = _split_bf16(w_ref[...])
    acc = _dot(ca_hi, w_hi) + _dot(ca_lo, w_hi) + _dot(ca_hi, w_lo)
    o_ref[...] = acc + b_ref[...]


def _mod_call(c, w3, b3, layer):
    bsz, d = c.shape
    n = w3.shape[-1]
    tn = MOD_TN
    return pl.pallas_call(
        _mod_kernel,
        grid=(n // tn,),
        in_specs=[
            pl.BlockSpec((bsz, d), lambda j: (0, 0)),
            pl.BlockSpec((None, d, tn), lambda j: (layer, 0, j)),
            pl.BlockSpec((None, 1, tn), lambda j: (layer, 0, j)),
        ],
        out_specs=pl.BlockSpec((bsz, tn), lambda j: (0, j)),
        out_shape=jax.ShapeDtypeStruct((bsz, n), F32),
        compiler_params=_params(("arbitrary",)),
        name="adaln_mod",
    )(c, w3, b3)


def _hgrn_kernel(x_ref, mod_ref, win_ref, wout_ref, lb_ref, ng_ref, tri_ref, o_ref,
                 proj_sc, bc_sc, y_sc, st_sc, *, ts, heads, layer):
    dk = HEAD_DIM
    w = heads * dk
    chunk = HGRN_CHUNK
    blk = HGRN_BLK
    nch = blk // chunk

    @pl.when(pl.program_id(1) == 0)
    def _():
        st_sc[...] = jnp.zeros_like(st_sc)

    x = x_ref[0]
    mod = mod_ref[0]
    h = _modulate(x, mod[0:1], mod[1:2]).astype(BF16)
    proj_sc[...] = _dot(h, win_ref[...])

    lb_e = jnp.exp(lb_ref[...] - jnp.max(lb_ref[...], axis=0, keepdims=True))
    lb = jnp.sum(lb_e[0:layer + 1], axis=0, keepdims=True) / jnp.sum(lb_e, axis=0, keepdims=True)
    fg = lb + (1.0 - lb) * _sigmoid(proj_sc[:, w:2 * w])
    proj_sc[:, w:2 * w] = fg
    logf_hi, logf_lo = _split_bf16(jnp.log(fg))
    tri = tri_ref[...]
    for rb in range(ts // blk):
        rows = slice(rb * blk, (rb + 1) * blk)
        bc_sc[rows] = _dot(tri, logf_hi[rows]) + _dot(tri, logf_lo[rows])

    row = lax.broadcasted_iota(jnp.int32, (blk, blk), 0)
    col = lax.broadcasted_iota(jnp.int32, (blk, blk), 1)
    mask = (row - col).astype(jnp.uint32) <= (row & (chunk - 1)).astype(jnp.uint32)

    def per_chunk(rows_1):
        return jnp.concatenate([jnp.broadcast_to(r, (chunk, dk)) for r in rows_1], axis=0)

    def chunk_rows(c):
        return slice(c * chunk, (c + 1) * chunk)

    def scores_stage(rb, hd):
        rows = slice(rb * blk, (rb + 1) * blk)
        bc = bc_sc[rows, hd * dk:(hd + 1) * dk]
        mid_rows = [bc[c * chunk + chunk // 2:c * chunk + chunk // 2 + 1] for c in range(nch)]
        last_rows = [bc[(c + 1) * chunk - 1:(c + 1) * chunk] for c in range(nch)]
        b_mid = per_chunk(mid_rows)
        q = proj_sc[rows, hd * dk:(hd + 1) * dk]
        q_mid = q * _sigmoid(q) * jnp.exp(bc - b_mid)
        k_mid = (1.0 - proj_sc[rows, w + hd * dk:w + (hd + 1) * dk]) * jnp.exp(b_mid - bc)
        q_inter = (q_mid * per_chunk([jnp.exp(r) for r in mid_rows])).astype(BF16)
        k_state = (k_mid * per_chunk([jnp.exp(l - r) for l, r in zip(last_rows, mid_rows)])).astype(BF16)
        v = proj_sc[rows, 2 * w + hd * dk:2 * w + (hd + 1) * dk].astype(BF16)
        s = _dot_nt(q_mid.astype(BF16), k_mid.astype(BF16))
        incs = [_dot_tn(v[chunk_rows(c)], k_state[chunk_rows(c)]) for c in range(nch)]
        return s, incs, q_inter, v, [jnp.exp(r) for r in last_rows]

    def state_stage(hd, s, incs, q_inter, v, decays):
        o_intra = _dot(jnp.where(mask, s, 0.0).astype(BF16), v)
        states = [st_sc[hd]]
        for c in range(nch):
            states.append(decays[c] * states[c] + incs[c])
        st_sc[hd] = states[nch]
        inter = [_dot_nt(q_inter[chunk_rows(c)], states[c].astype(BF16)) for c in range(nch)]
        return o_intra, inter

    def output_stage(rb, hd, o_intra, inter):
        rows = slice(rb * blk, (rb + 1) * blk)
        cs = slice(hd * dk, (hd + 1) * dk)
        o = jnp.concatenate([o_intra[chunk_rows(c)] + inter[c] for c in range(nch)], axis=0)
        g = proj_sc[rows, 3 * w + hd * dk:3 * w + (hd + 1) * dk]
        y_sc[rows, cs] = (_rms(o) * (g * _sigmoid(g) * ng_ref[:, cs])).astype(BF16)

    items = [(rb, hd) for rb in range(ts // blk) for hd in range(heads)]
    stage1, stage2 = {}, {}
    for n in range(len(items) + 2):
        if n < len(items):
            stage1[n] = scores_stage(*items[n])
        if 1 <= n <= len(items):
            stage2[n - 1] = state_stage(items[n - 1][1], *stage1.pop(n - 1))
        if n >= 2:
            output_stage(*items[n - 2], *stage2.pop(n - 2))

    o_ref[0] = x + mod[2:3] * _dot(y_sc[...], wout_ref[...])


def _hgrn_call(x, mod, w_in, w_out, lb_logits, layer, norm_g):
    bsz, seq, d = x.shape
    ts = HGRN_TS
    heads = w_out.shape[0] // HEAD_DIM
    w = heads * HEAD_DIM
    n_lb = lb_logits.shape[0]
    idx = jnp.arange(HGRN_BLK)
    tri = ((idx[:, None] >= idx[None, :]) & (idx[:, None] // HGRN_CHUNK == idx[None, :] // HGRN_CHUNK)).astype(BF16)
    kern = functools.partial(_hgrn_kernel, ts=ts, heads=heads, layer=layer)
    return pl.pallas_call(
        kern,
        grid=(bsz, seq // ts),
        in_specs=[
            pl.BlockSpec((1, ts, d), lambda b, s: (b, s, 0)),
            pl.BlockSpec((1, 6, d), lambda b, s: (b, 0, 0)),
            _resident((d, 4 * w), lambda b, s: (0, 0)),
            _resident((w, d), lambda b, s: (0, 0)),
            _resident((n_lb, w), lambda b, s: (0, 0)),
            _resident((1, w), lambda b, s: (0, 0)),
            _resident((HGRN_BLK, HGRN_BLK), lambda b, s: (0, 0)),
        ],
        out_specs=pl.BlockSpec((1, ts, d), lambda b, s: (b, s, 0)),
        out_shape=jax.ShapeDtypeStruct(x.shape, F32),
        scratch_shapes=[
            pltpu.VMEM((ts, 4 * w), F32),
            pltpu.VMEM((ts, w), F32),
            pltpu.VMEM((ts, w), BF16),
            pltpu.VMEM((heads, HEAD_DIM, HEAD_DIM), F32),
        ],
        compiler_params=_params(("arbitrary", "arbitrary")),
        name="hgrn2_layer",
    )(x, mod, w_in.astype(BF16), w_out.astype(BF16), lb_logits.astype(F32),
      jnp.tile(norm_g, heads).reshape(1, w), tri)


def _ffn_kernel(x_ref, mod_ref, wup_ref, cw_ref, cb_ref, wdn_ref, o_ref, tail_sc, perm_sc, act_sc, *, tm, ff):
    fc = FFN_FC
    d = x_ref.shape[-1]
    nj = tm // 8
    n_slab = d // 128

    @pl.when(pl.program_id(1) == 0)
    def _():
        tail_sc[...] = jnp.zeros_like(tail_sc)

    def perm_rows(jn):
        g = nj // 8
        return pl.ds((jn % g) * 64 + jn // g, 8, stride=8)

    x = x_ref[0]
    mod = mod_ref[0]
    h_nat = _modulate(x, mod[3:4], mod[4:5])
    for sl in range(n_slab):
        for jn in range(nj):
            perm_sc[sl, perm_rows(jn), :] = h_nat[8 * jn:8 * jn + 8, sl * 128:(sl + 1) * 128]
    h = jnp.concatenate([perm_sc[sl] for sl in range(n_slab)], axis=1).astype(BF16)

    first_row = lax.broadcasted_iota(jnp.int32, (8, fc), 0) == 0

    def conv_cols(c0):
        u = _dot(h, wup_ref[:, c0:c0 + fc])
        prev = tail_sc[:, c0:c0 + fc]
        tail_sc[:, c0:c0 + fc] = u[tm - 16:tm]
        back1 = jnp.where(first_row, pltpu.roll(prev[8:16], 1, 0), pltpu.roll(u[tm - 8:tm], 1, 0))
        back2 = jnp.where(first_row, pltpu.roll(prev[0:8], 1, 0), pltpu.roll(u[tm - 16:tm - 8], 1, 0))
        u1 = jnp.concatenate([back1, u[0:tm - 8]], axis=0)
        u2 = jnp.concatenate([back2, back1, u[0:tm - 16]], axis=0)
        cw = cw_ref[:, c0:c0 + fc]
        return cw[2:3] * u + cw[1:2] * u1 + cw[0:1] * u2 + cb_ref[:, c0:c0 + fc]

    for j in range(ff // fc):
        gate = conv_cols(j * fc)
        val = conv_cols(ff + j * fc)
        act_sc[:, j * fc:(j + 1) * fc] = (gate * _sigmoid(gate) * val).astype(BF16)
    y = _dot(act_sc[...], wdn_ref[...])

    for sl in range(n_slab):
        perm_sc[sl] = y[:, sl * 128:(sl + 1) * 128]
    y_nat = jnp.concatenate(
        [jnp.concatenate([perm_sc[sl, perm_rows(jn), :] for jn in range(nj)], axis=0) for sl in range(n_slab)],
        axis=1)
    o_ref[0] = x + mod[5:6] * y_nat


def _ffn_call(x, mod, w_up, conv_w, conv_b, w_down):
    bsz, seq, d = x.shape
    ff = w_down.shape[0]
    tm = FFN_TM
    kern = functools.partial(_ffn_kernel, tm=tm, ff=ff)
    return pl.pallas_call(
        kern,
        grid=(bsz, seq // tm),
        in_specs=[
            pl.BlockSpec((1, tm, d), lambda b, s: (b, s, 0)),
            pl.BlockSpec((1, 6, d), lambda b, s: (b, 0, 0)),
            _resident((d, 2 * ff), lambda b, s: (0, 0)),
            _resident((CONV_W, 2 * ff), lambda b, s: (0, 0)),
            _resident((1, 2 * ff), lambda b, s: (0, 0)),
            _resident((ff, d), lambda b, s: (0, 0)),
        ],
        out_specs=pl.BlockSpec((1, tm, d), lambda b, s: (b, s, 0)),
        out_shape=jax.ShapeDtypeStruct(x.shape, F32),
        scratch_shapes=[
            pltpu.VMEM((16, 2 * ff), F32),
            pltpu.VMEM((d // 128, tm, 128), F32),
            pltpu.VMEM((tm, ff), BF16),
        ],
        compiler_params=_params(("arbitrary", "arbitrary")),
        name="conv_glu_ffn",
    )(x, mod, w_up.astype(BF16), conv_w, conv_b.reshape(1, 2 * ff), w_down.astype(BF16))


def _kv_kernel(x_ref, mod_ref, wkv_ref, wf_ref, bf_ref, kg_ref, tri_ref, k_ref, v_ref, f_ref, carry_sc,
               *, ts, heads):
    dh = HEAD_DIM
    w = heads * dh

    @pl.when(pl.program_id(1) == 0)
    def _():
        carry_sc[...] = jnp.zeros_like(carry_sc)

    mod = mod_ref[0]
    h = _modulate(x_ref[0], mod[0:1], mod[1:2]).astype(BF16)
    kg = kg_ref[...]
    half = w // 2

    def keys(c0):
        kk = _dot(h, wkv_ref[:, c0:c0 + half])
        for hd in range(half // dh):
            cs = slice(hd * dh, (hd + 1) * dh)
            k_ref[0, :, c0 + hd * dh:c0 + (hd + 1) * dh] = (_rms(kk[:, cs]) * kg).astype(BF16)

    def values(c0):
        v_ref[0, :, c0:c0 + half] = _dot(h, wkv_ref[:, w + c0:w + c0 + half]).astype(BF16)

    z = _dot(h, wf_ref[...]) + bf_ref[...]
    keys(0)
    log_f = jnp.minimum(z, 0.0) - jnp.log(1.0 + jnp.exp(-jnp.abs(z)))
    hi, lo = _split_bf16(log_f)
    tri = tri_ref[...]
    cum = _dot(tri, hi) + _dot(tri, lo) + carry_sc[0:1, :]
    keys(half)
    carry_sc[...] = jnp.broadcast_to(cum[ts - 1:ts, :], carry_sc.shape)
    values(0)
    t0 = -LOG2E * cum
    b0 = t0.astype(BF16).astype(F32)
    t1 = t0 - b0
    b1 = t1.astype(BF16).astype(F32)
    b2 = t1 - b1
    sub = lax.broadcasted_iota(jnp.int32, (ts, dh), 1) & (BIAS_LANES - 1)
    f_ref[0] = jnp.where(sub == 0, b0, jnp.where(sub == 1, b1, jnp.where(sub == 2, b2, 0.0))).astype(BF16)
    values(half)


def _kv_call(x, mod, kv_w, kv_b_f, k_norm_g):
    bsz, seq, d = x.shape
    heads = kv_b_f.shape[0]
    w = heads * HEAD_DIM
    ts = KV_TS
    assert heads * BIAS_LANES <= HEAD_DIM
    idx = jnp.arange(ts)
    tri = (idx[:, None] >= idx[None, :]).astype(BF16)
    lane = jnp.arange(HEAD_DIM)
    head_of_lane = jnp.minimum(lane // BIAS_LANES, heads - 1)
    used = (lane // BIAS_LANES < heads) & (lane % BIAS_LANES < 3)
    wf = jnp.where(used[None, :], kv_w[:, 2 * w:][:, head_of_lane], 0.0).astype(BF16)
    bf = jnp.where(used, kv_b_f.astype(F32)[head_of_lane], 0.0).reshape(1, HEAD_DIM)
    kern = functools.partial(_kv_kernel, ts=ts, heads=heads)
    return pl.pallas_call(
        kern,
        grid=(bsz, seq // ts),
        in_specs=[
            pl.BlockSpec((1, ts, d), lambda b, s: (b, s, 0)),
            pl.BlockSpec((1, 2, d), lambda b, s: (b, 0, 0)),
            _resident((d, 2 * w), lambda b, s: (0, 0)),
            _resident((d, HEAD_DIM), lambda b, s: (0, 0)),
            _resident((1, HEAD_DIM), lambda b, s: (0, 0)),
            _resident((1, HEAD_DIM), lambda b, s: (0, 0)),
            _resident((ts, ts), lambda b, s: (0, 0)),
        ],
        out_specs=[
            pl.BlockSpec((1, ts, w), lambda b, s: (b, s, 0)),
            pl.BlockSpec((1, ts, w), lambda b, s: (b, s, 0)),
            pl.BlockSpec((1, ts, HEAD_DIM), lambda b, s: (b, s, 0)),
        ],
        out_shape=[
            jax.ShapeDtypeStruct((bsz, seq, w), BF16),
            jax.ShapeDtypeStruct((bsz, seq, w), BF16),
            jax.ShapeDtypeStruct((bsz, seq, HEAD_DIM), BF16),
        ],
        scratch_shapes=[pltpu.VMEM((8, HEAD_DIM), F32)],
        compiler_params=_params(("arbitrary", "arbitrary")),
        name="shared_kv",
    )(x, mod, kv_w[:, :2 * w].astype(BF16), wf, bf, k_norm_g.reshape(1, HEAD_DIM), tri)


def _fox_kernel(x_ref, mod_ref, wq_ref, qg_ref, k_ref, v_ref, f_ref, wout_ref, o_ref,
                q_sc, gate_sc, y_sc, m_sc, acc_sc, *, tq, tk, heads):
    dh = HEAD_DIM
    w = heads * dh
    qi = pl.program_id(1)

    x = x_ref[0]
    mod = mod_ref[0]
    h = _modulate(x, mod[0:1], mod[1:2]).astype(BF16)
    pq = _dot(h, wq_ref[...])
    gate_sc[...] = _sigmoid(pq[:, w:2 * w])
    qg = qg_ref[...] * (dh ** -0.5 * LOG2E)
    lane = lax.broadcasted_iota(jnp.int32, (tq, dh), 1)
    for hd in range(heads):
        cs = slice(hd * dh, (hd + 1) * dh)
        q_sc[hd, :, 0:dh] = (_rms(pq[:, cs]) * qg).astype(BF16)
        picks_bias = (lane - hd * BIAS_LANES).astype(jnp.uint32) < 3
        q_sc[hd, :, dh:2 * dh] = jnp.where(picks_bias, 1.0, 0.0).astype(BF16)
    m_sc[...] = jnp.full(m_sc.shape, NEG_INF, F32)
    acc_sc[...] = jnp.zeros(acc_sc.shape, F32)

    def step(ks, width, masked):
        f_tile = f_ref[0, pl.ds(ks, width), :]
        ones = jnp.ones((width, dh), BF16)
        if masked:
            row = lax.broadcasted_iota(jnp.int32, (tq, width), 0)
            col = lax.broadcasted_iota(jnp.int32, (tq, width), 1)
            causal = col <= row

        def logits(hd):
            k_aug = jnp.concatenate([k_ref[0, pl.ds(ks, width), hd * dh:(hd + 1) * dh], f_tile], axis=1)
            return _dot_nt(q_sc[hd], k_aug)

        def weights(hd, s):
            if masked:
                s = jnp.where(causal, s, NEG_INF)
            m_prev = m_sc[hd]
            m_new = jnp.maximum(m_prev, jnp.broadcast_to(jnp.max(s, axis=-1, keepdims=True), (tq, dh)))
            m_sc[hd] = m_new
            alpha = jnp.exp2(m_prev - m_new)
            p = jnp.exp2(s - jnp.concatenate([m_new] * (width // dh), axis=1)).astype(BF16)
            v_aug = jnp.concatenate([v_ref[0, pl.ds(ks, width), hd * dh:(hd + 1) * dh], ones], axis=1)
            return alpha, _dot(p, v_aug)

        def accumulate(hd, alpha, pv):
            acc_sc[hd] = jnp.concatenate([alpha, alpha], axis=1) * acc_sc[hd] + pv

        stage1, stage2 = {}, {}
        for n in range(heads + 2):
            if n < heads:
                stage1[n] = logits(n)
            if 1 <= n <= heads:
                stage2[n - 1] = weights(n - 1, stage1.pop(n - 1))
            if n >= 2:
                accumulate(n - 2, *stage2.pop(n - 2))

    q0 = qi * tq
    n_wide = q0 // tk

    def body(j, carry):
        step(pl.multiple_of(j * tk, tk), tk, False)
        return carry

    lax.fori_loop(0, n_wide, body, 0)
    for r in range(tk // tq - 1):
        @pl.when(n_wide * tk + r * tq < q0)
        def _():
            step(pl.multiple_of(n_wide * tk + r * tq, tq), tq, False)
    step(pl.multiple_of(q0, tq), tq, True)

    for hd in range(heads):
        cs = slice(hd * dh, (hd + 1) * dh)
        acc = acc_sc[hd]
        y_sc[:, cs] = (acc[:, 0:dh] / acc[:, dh:2 * dh] * gate_sc[:, cs]).astype(BF16)

    o_ref[0] = x + mod[2:3] * _dot(y_sc[...], wout_ref[...])


def _fox_call(x, mod, w_q, q_norm_g, w_out, k, v, fbias):
    bsz, seq, d = x.shape
    w = w_out.shape[0]
    heads = w // HEAD_DIM
    tq, tk = ATT_TQ, ATT_TK
    kern = functools.partial(_fox_kernel, tq=tq, tk=tk, heads=heads)
    return pl.pallas_call(
        kern,
        grid=(bsz, seq // tq),
        in_specs=[
            pl.BlockSpec((1, tq, d), lambda b, s: (b, s, 0)),
            pl.BlockSpec((1, 6, d), lambda b, s: (b, 0, 0)),
            _resident((d, 2 * w), lambda b, s: (0, 0)),
            _resident((1, HEAD_DIM), lambda b, s: (0, 0)),
            _resident((1, seq, w), lambda b, s: (b, 0, 0)),
            _resident((1, seq, w), lambda b, s: (b, 0, 0)),
            _resident((1, seq, HEAD_DIM), lambda b, s: (b, 0, 0)),
            _resident((w, d), lambda b, s: (0, 0)),
        ],
        out_specs=pl.BlockSpec((1, tq, d), lambda b, s: (b, s, 0)),
        out_shape=jax.ShapeDtypeStruct(x.shape, F32),
        scratch_shapes=[
            pltpu.VMEM((heads, tq, 2 * HEAD_DIM), BF16),
            pltpu.VMEM((tq, w), F32),
            pltpu.VMEM((tq, w), BF16),
            pltpu.VMEM((heads, tq, HEAD_DIM), F32),
            pltpu.VMEM((heads, tq, 2 * HEAD_DIM), F32),
        ],
        compiler_params=_params(("arbitrary", "arbitrary")),
        name="fox_layer",
    )(x, mod, w_q.astype(BF16), q_norm_g.reshape(1, HEAD_DIM), k, v, fbias, w_out.astype(BF16))


def kernel(x, c, ada_w, ada_b, a_w_in, a_lb_logits, a_norm_g, a_w_out, kv_ada_w, kv_ada_b, kv_w, kv_b_f, k_norm_g,
           b_w_q, q_norm_g, b_w_out, ffn_w_up, ffn_conv_w, ffn_conv_b, ffn_w_down):
    bsz, seq, d = x.shape
    depth = ada_w.shape[0]
    n_a = a_w_in.shape[0]
    ada_b3 = ada_b.reshape(depth, 1, 6 * d)
    k_sh = v_sh = fbias = None
    for l in range(depth):
        mod = _mod_call(c, ada_w, ada_b3, l).reshape(bsz, 6, d)
        if l == n_a:
            kv_mod = _mod_call(c, kv_ada_w[None], kv_ada_b.reshape(1, 1, 2 * d), 0).reshape(bsz, 2, d)
            k_sh, v_sh, fbias = _kv_call(x, kv_mod, kv_w, kv_b_f, k_norm_g)
        if l < n_a:
            x = _hgrn_call(x, mod, a_w_in[l], a_w_out[l], a_lb_logits, l, a_norm_g[l])
        else:
            j = l - n_a
            x = _fox_call(x, mod, b_w_q[j], q_norm_g[j], b_w_out[j], k_sh, v_sh, fbias)
        x = _ffn_call(x, mod, ffn_w_up[l], ffn_conv_w[l], ffn_conv_b[l], ffn_w_down[l])
    return x
```

```python
import functools

import jax
import jax.numpy as jnp
from jax import lax
from jax.experimental import pallas as pl
from jax.experimental.pallas import tpu as pltpu

F32 = jnp.float32
BF16 = jnp.bfloat16

EPS = 1e-6
NEG_INF = -1e30
HEAD_DIM = 128
HGRN_CHUNK = 64
CONV_W = 3
LOG2E = 1.4426950408889634
BIAS_LANES = 16

V7X_VMEM_BYTES = 64 * 1024 * 1024
VMEM_LIMIT_BYTES = V7X_VMEM_BYTES - 8 * 1024 * 1024

MOD_TN = 1024
HGRN_TS = 512
HGRN_BLK = 256
FFN_TM = 1024
FFN_FC = 256
KV_TS = 512
ATT_TQ = 512
ATT_DIAG = 256


def _dot(a, b):
    return jnp.dot(a, b, preferred_element_type=F32)


def _dot_nt(a, b):
    return lax.dot_general(a, b, (((1,), (1,)), ((), ())), preferred_element_type=F32)


def _dot_tn(a, b):
    return lax.dot_general(a, b, (((0,), (0,)), ((), ())), preferred_element_type=F32)


def _sigmoid(x):
    return 1.0 / (1.0 + jnp.exp(-x))


def _rms(x):
    return x * lax.rsqrt(jnp.mean(x * x, axis=-1, keepdims=True) + EPS)


def _modulate(x, shift, scale):
    return _rms(x) * (1.0 + scale) + shift


def _split_bf16(x):
    hi = x.astype(BF16)
    lo = (x - hi.astype(F32)).astype(BF16)
    return hi, lo


def _resident(block_shape, index_map):
    return pl.BlockSpec(block_shape, index_map, pipeline_mode=pl.Buffered(1))


def _params(semantics):
    return pltpu.CompilerParams(dimension_semantics=semantics, vmem_limit_bytes=VMEM_LIMIT_BYTES)


def _mod_kernel(c_ref, w_ref, b_ref, o_ref):
    c = c_ref[...]
    ca_hi, ca_lo = _split_bf16(c * _sigmoid(c))
    w_hi, w_lo = _split_bf16(w_ref[...])
    acc = _dot(ca_hi, w_hi) + _dot(ca_lo, w_hi) + _dot(ca_hi, w_lo)
    o_ref[...] = acc + b_ref[...]


def _mod_call(c, w3, b3, layer):
    bsz, d = c.shape
    n = w3.shape[-1]
    tn = MOD_TN
    return pl.pallas_call(
        _mod_kernel,
        grid=(n // tn,),
        in_specs=[
            pl.BlockSpec((bsz, d), lambda j: (0, 0)),
            pl.BlockSpec((None, d, tn), lambda j: (layer, 0, j)),
            pl.BlockSpec((None, 1, tn), lambda j: (layer, 0, j)),
        ],
        out_specs=pl.BlockSpec((bsz, tn), lambda j: (0, j)),
        out_shape=jax.ShapeDtypeStruct((bsz, n), F32),
        compiler_params=_params(("arbitrary",)),
        name="adaln_mod",
    )(c, w3, b3)


def _hgrn_kernel(x_ref, mod_ref, win_ref, wout_ref, lb_ref, ng_ref, tri_ref, o_ref,
                 proj_sc, bc_sc, y_sc, st_sc, *, ts, heads, layer):
    dk = HEAD_DIM
    w = heads * dk
    chunk = HGRN_CHUNK
    blk = HGRN_BLK
    nch = blk // chunk

    @pl.when(pl.program_id(1) == 0)
    def _():
        st_sc[...] = jnp.zeros_like(st_sc)

    x = x_ref[0]
    mod = mod_ref[0]
    h = _modulate(x, mod[0:1], mod[1:2]).astype(BF16)
    proj_sc[...] = _dot(h, win_ref[...])

    lb_e = jnp.exp(lb_ref[...] - jnp.max(lb_ref[...], axis=0, keepdims=True))
    lb = jnp.sum(lb_e[0:layer + 1], axis=0, keepdims=True) / jnp.sum(lb_e, axis=0, keepdims=True)
    fg = lb + (1.0 - lb) * _sigmoid(proj_sc[:, w:2 * w])
    proj_sc[:, w:2 * w] = fg
    logf_hi, logf_lo = _split_bf16(jnp.log(fg))
    tri = tri_ref[...]
    for rb in range(ts // blk):
        rows = slice(rb * blk, (rb + 1) * blk)
        bc_sc[rows] = _dot(tri, logf_hi[rows]) + _dot(tri, logf_lo[rows])

    row = lax.broadcasted_iota(jnp.int32, (blk, blk), 0)
    col = lax.broadcasted_iota(jnp.int32, (blk, blk), 1)
    mask = (row - col).astype(jnp.uint32) <= (row & (chunk - 1)).astype(jnp.uint32)

    def per_chunk(rows_1):
        return jnp.concatenate([jnp.broadcast_to(r, (chunk, dk)) for r in rows_1], axis=0)

    def chunk_rows(c):
        return slice(c * chunk, (c + 1) * chunk)

    def scores_stage(rb, hd):
        rows = slice(rb * blk, (rb + 1) * blk)
        bc = bc_sc[rows, hd * dk:(hd + 1) * dk]
        mid_rows = [bc[c * chunk + chunk // 2:c * chunk + chunk // 2 + 1] for c in range(nch)]
        last_rows = [bc[(c + 1) * chunk - 1:(c + 1) * chunk] for c in range(nch)]
        b_mid = per_chunk(mid_rows)
        q = proj_sc[rows, hd * dk:(hd + 1) * dk]
        q_mid = q * _sigmoid(q) * jnp.exp(bc - b_mid)
        k_mid = (1.0 - proj_sc[rows, w + hd * dk:w + (hd + 1) * dk]) * jnp.exp(b_mid - bc)
        q_inter = (q_mid * per_chunk([jnp.exp(r) for r in mid_rows])).astype(BF16)
        k_state = (k_mid * per_chunk([jnp.exp(l - r) for l, r in zip(last_rows, mid_rows)])).astype(BF16)
        v = proj_sc[rows, 2 * w + hd * dk:2 * w + (hd + 1) * dk].astype(BF16)
        s = _dot_nt(q_mid.astype(BF16), k_mid.astype(BF16))
        incs = [_dot_tn(v[chunk_rows(c)], k_state[chunk_rows(c)]) for c in range(nch)]
        return s, incs, q_inter, v, [jnp.exp(r) for r in last_rows]

    def state_stage(hd, s, incs, q_inter, v, decays):
        o_intra = _dot(jnp.where(mask, s, 0.0).astype(BF16), v)
        states = [st_sc[hd]]
        for c in range(nch):
            states.append(decays[c] * states[c] + incs[c])
        st_sc[hd] = states[nch]
        inter = [_dot_nt(q_inter[chunk_rows(c)], states[c].astype(BF16)) for c in range(nch)]
        return o_intra, inter

    def output_stage(rb, hd, o_intra, inter):
        rows = slice(rb * blk, (rb + 1) * blk)
        cs = slice(hd * dk, (hd + 1) * dk)
        o = jnp.concatenate([o_intra[chunk_rows(c)] + inter[c] for c in range(nch)], axis=0)
        g = proj_sc[rows, 3 * w + hd * dk:3 * w + (hd + 1) * dk]
        y_sc[rows, cs] = (_rms(o) * (g * _sigmoid(g) * ng_ref[:, cs])).astype(BF16)

    items = [(rb, hd) for rb in range(ts // blk) for hd in range(heads)]
    stage1, stage2 = {}, {}
    for n in range(len(items) + 2):
        if n < len(items):
            stage1[n] = scores_stage(*items[n])
        if 1 <= n <= len(items):
            stage2[n - 1] = state_stage(items[n - 1][1], *stage1.pop(n - 1))
        if n >= 2:
            output_stage(*items[n - 2], *stage2.pop(n - 2))

    o_ref[0] = x + mod[2:3] * _dot(y_sc[...], wout_ref[...])


def _hgrn_call(x, mod, w_in, w_out, lb_logits, layer, norm_g):
    bsz, seq, d = x.shape
    ts = HGRN_TS
    heads = w_out.shape[0] // HEAD_DIM
    w = heads * HEAD_DIM
    n_lb = lb_logits.shape[0]
    idx = jnp.arange(HGRN_BLK)
    tri = ((idx[:, None] >= idx[None, :]) & (idx[:, None] // HGRN_CHUNK == idx[None, :] // HGRN_CHUNK)).astype(BF16)
    kern = functools.partial(_hgrn_kernel, ts=ts, heads=heads, layer=layer)
    return pl.pallas_call(
        kern,
        grid=(bsz, seq // ts),
        in_specs=[
            pl.BlockSpec((1, ts, d), lambda b, s: (b, s, 0)),
            pl.BlockSpec((1, 6, d), lambda b, s: (b, 0, 0)),
            _resident((d, 4 * w), lambda b, s: (0, 0)),
            _resident((w, d), lambda b, s: (0, 0)),
            _resident((n_lb, w), lambda b, s: (0, 0)),
            _resident((1, w), lambda b, s: (0, 0)),
            _resident((HGRN_BLK, HGRN_BLK), lambda b, s: (0, 0)),
        ],
        out_specs=pl.BlockSpec((1, ts, d), lambda b, s: (b, s, 0)),
        out_shape=jax.ShapeDtypeStruct(x.shape, F32),
        scratch_shapes=[
            pltpu.VMEM((ts, 4 * w), F32),
            pltpu.VMEM((ts, w), F32),
            pltpu.VMEM((ts, w), BF16),
            pltpu.VMEM((heads, HEAD_DIM, HEAD_DIM), F32),
        ],
        compiler_params=_params(("arbitrary", "arbitrary")),
        name="hgrn2_layer",
    )(x, mod, w_in.astype(BF16), w_out.astype(BF16), lb_logits.astype(F32),
      jnp.tile(norm_g, heads).reshape(1, w), tri)


def _ffn_kernel(x_ref, mod_ref, wup_ref, cw_ref, cb_ref, wdn_ref, o_ref, tail_sc, perm_sc, act_sc, *, tm, ff):
    fc = FFN_FC
    d = x_ref.shape[-1]
    nj = tm // 8
    n_slab = d // 128

    @pl.when(pl.program_id(1) == 0)
    def _():
        tail_sc[...] = jnp.zeros_like(tail_sc)

    def perm_rows(jn):
        g = nj // 8
        return pl.ds((jn % g) * 64 + jn // g, 8, stride=8)

    x = x_ref[0]
    mod = mod_ref[0]
    h_nat = _modulate(x, mod[3:4], mod[4:5])
    for sl in range(n_slab):
        for jn in range(nj):
            perm_sc[sl, perm_rows(jn), :] = h_nat[8 * jn:8 * jn + 8, sl * 128:(sl + 1) * 128]
    h = jnp.concatenate([perm_sc[sl] for sl in range(n_slab)], axis=1).astype(BF16)

    first_row = lax.broadcasted_iota(jnp.int32, (8, fc), 0) == 0

    def conv_cols(c0):
        u = _dot(h, wup_ref[:, c0:c0 + fc])
        prev = tail_sc[:, c0:c0 + fc]
        tail_sc[:, c0:c0 + fc] = u[tm - 16:tm]
        back1 = jnp.where(first_row, pltpu.roll(prev[8:16], 1, 0), pltpu.roll(u[tm - 8:tm], 1, 0))
        back2 = jnp.where(first_row, pltpu.roll(prev[0:8], 1, 0), pltpu.roll(u[tm - 16:tm - 8], 1, 0))
        u1 = jnp.concatenate([back1, u[0:tm - 8]], axis=0)
        u2 = jnp.concatenate([back2, back1, u[0:tm - 16]], axis=0)
        cw = cw_ref[:, c0:c0 + fc]
        return cw[2:3] * u + cw[1:2] * u1 + cw[0:1] * u2 + cb_ref[:, c0:c0 + fc]

    for j in range(ff // fc):
        gate = conv_cols(j * fc)
        val = conv_cols(ff + j * fc)
        act_sc[:, j * fc:(j + 1) * fc] = (gate * _sigmoid(gate) * val).astype(BF16)
    y = _dot(act_sc[...], wdn_ref[...])

    for sl in range(n_slab):
        perm_sc[sl] = y[:, sl * 128:(sl + 1) * 128]
    y_nat = jnp.concatenate(
        [jnp.concatenate([perm_sc[sl, perm_rows(jn), :] for jn in range(nj)], axis=0) for sl in range(n_slab)],
        axis=1)
    o_ref[0] = x + mod[5:6] * y_nat


def _ffn_call(x, mod, w_up, conv_w, conv_b, w_down):
    bsz, seq, d = x.shape
    ff = w_down.shape[0]
    tm = FFN_TM
    kern = functools.partial(_ffn_kernel, tm=tm, ff=ff)
    return pl.pallas_call(
        kern,
        grid=(bsz, seq // tm),
        in_specs=[
            pl.BlockSpec((1, tm, d), lambda b, s: (b, s, 0)),
            pl.BlockSpec((1, 6, d), lambda b, s: (b, 0, 0)),
            _resident((d, 2 * ff), lambda b, s: (0, 0)),
            _resident((CONV_W, 2 * ff), lambda b, s: (0, 0)),
            _resident((1, 2 * ff), lambda b, s: (0, 0)),
            _resident((ff, d), lambda b, s: (0, 0)),
        ],
        out_specs=pl.BlockSpec((1, tm, d), lambda b, s: (b, s, 0)),
        out_shape=jax.ShapeDtypeStruct(x.shape, F32),
        scratch_shapes=[
            pltpu.VMEM((16, 2 * ff), F32),
            pltpu.VMEM((d // 128, tm, 128), F32),
            pltpu.VMEM((tm, ff), BF16),
        ],
        compiler_params=_params(("arbitrary", "arbitrary")),
        name="conv_glu_ffn",
    )(x, mod, w_up.astype(BF16), conv_w, conv_b.reshape(1, 2 * ff), w_down.astype(BF16))


def _kv_kernel(x_ref, mod_ref, wkv_ref, wf_ref, bf_ref, kg_ref, tri_ref, k_ref, v_ref, f_ref, carry_sc,
               *, ts, heads):
    dh = HEAD_DIM
    w = heads * dh

    @pl.when(pl.program_id(1) == 0)
    def _():
        carry_sc[...] = jnp.zeros_like(carry_sc)

    mod = mod_ref[0]
    h = _modulate(x_ref[0], mod[0:1], mod[1:2]).astype(BF16)
    kg = kg_ref[...]
    half = w // 2

    def keys(c0):
        kk = _dot(h, wkv_ref[:, c0:c0 + half])
        for hd in range(half // dh):
            cs = slice(hd * dh, (hd + 1) * dh)
            k_ref[0, :, c0 + hd * dh:c0 + (hd + 1) * dh] = (_rms(kk[:, cs]) * kg).astype(BF16)

    def values(c0):
        v_ref[0, :, c0:c0 + half] = _dot(h, wkv_ref[:, w + c0:w + c0 + half]).astype(BF16)

    z = _dot(h, wf_ref[...]) + bf_ref[...]
    keys(0)
    log_f = jnp.minimum(z, 0.0) - jnp.log(1.0 + jnp.exp(-jnp.abs(z)))
    hi, lo = _split_bf16(log_f)
    tri = tri_ref[...]
    cum = _dot(tri, hi) + _dot(tri, lo) + carry_sc[0:1, :]
    keys(half)
    carry_sc[...] = jnp.broadcast_to(cum[ts - 1:ts, :], carry_sc.shape)
    values(0)
    t0 = -LOG2E * cum
    b0 = t0.astype(BF16).astype(F32)
    t1 = t0 - b0
    b1 = t1.astype(BF16).astype(F32)
    b2 = t1 - b1
    sub = lax.broadcasted_iota(jnp.int32, (ts, dh), 1) & (BIAS_LANES - 1)
    f_ref[0] = jnp.where(sub == 0, b0, jnp.where(sub == 1, b1, jnp.where(sub == 2, b2, 0.0))).astype(BF16)
    values(half)


def _kv_call(x, mod, kv_w, kv_b_f, k_norm_g):
    bsz, seq, d = x.shape
    heads = kv_b_f.shape[0]
    w = heads * HEAD_DIM
    ts = KV_TS
    assert heads * BIAS_LANES <= HEAD_DIM
    idx = jnp.arange(ts)
    tri = (idx[:, None] >= idx[None, :]).astype(BF16)
    lane = jnp.arange(HEAD_DIM)
    head_of_lane = jnp.minimum(lane // BIAS_LANES, heads - 1)
    used = (lane // BIAS_LANES < heads) & (lane % BIAS_LANES < 3)
    wf = jnp.where(used[None, :], kv_w[:, 2 * w:][:, head_of_lane], 0.0).astype(BF16)
    bf = jnp.where(used, kv_b_f.astype(F32)[head_of_lane], 0.0).reshape(1, HEAD_DIM)
    kern = functools.partial(_kv_kernel, ts=ts, heads=heads)
    return pl.pallas_call(
        kern,
        grid=(bsz, seq // ts),
        in_specs=[
            pl.BlockSpec((1, ts, d), lambda b, s: (b, s, 0)),
            pl.BlockSpec((1, 2, d), lambda b, s: (b, 0, 0)),
            _resident((d, 2 * w), lambda b, s: (0, 0)),
            _resident((d, HEAD_DIM), lambda b, s: (0, 0)),
            _resident((1, HEAD_DIM), lambda b, s: (0, 0)),
            _resident((1, HEAD_DIM), lambda b, s: (0, 0)),
            _resident((ts, ts), lambda b, s: (0, 0)),
        ],
        out_specs=[
            pl.BlockSpec((1, ts, w), lambda b, s: (b, s, 0)),
            pl.BlockSpec((1, ts, w), lambda b, s: (b, s, 0)),
            pl.BlockSpec((1, ts, HEAD_DIM), lambda b, s: (b, s, 0)),
        ],
        out_shape=[
            jax.ShapeDtypeStruct((bsz, seq, w), BF16),
            jax.ShapeDtypeStruct((bsz, seq, w), BF16),
            jax.ShapeDtypeStruct((bsz, seq, HEAD_DIM), BF16),
        ],
        scratch_shapes=[pltpu.VMEM((8, HEAD_DIM), F32)],
        compiler_params=_params(("arbitrary", "arbitrary")),
        name="shared_kv",
    )(x, mod, kv_w[:, :2 * w].astype(BF16), wf, bf, k_norm_g.reshape(1, HEAD_DIM), tri)


def _fox_kernel(x_ref, mod_ref, wq_ref, qg_ref, k_ref, v_ref, f_ref, wout_ref, o_ref,
                q_sc, gate_sc, y_sc, m_sc, acc_sc, k_all, v_all, f_all, *, tq, heads):
    dh = HEAD_DIM
    w = heads * dh
    qi = pl.program_id(1)
    q0 = pl.multiple_of(qi * tq, tq)
    k_all[pl.ds(q0, tq), :] = k_ref[0]
    v_all[pl.ds(q0, tq), :] = v_ref[0]
    f_all[pl.ds(q0, tq), :] = f_ref[0]

    x = x_ref[0]
    mod = mod_ref[0]
    h = _modulate(x, mod[0:1], mod[1:2]).astype(BF16)
    pq = _dot(h, wq_ref[...])
    gate_sc[...] = _sigmoid(pq[:, w:2 * w])
    qg = qg_ref[...] * (dh ** -0.5 * LOG2E)
    lane = lax.broadcasted_iota(jnp.int32, (tq, dh), 1)
    for hd in range(heads):
        cs = slice(hd * dh, (hd + 1) * dh)
        q_sc[hd, :, 0:dh] = (_rms(pq[:, cs]) * qg).astype(BF16)
        picks_bias = (lane - hd * BIAS_LANES).astype(jnp.uint32) < 3
        q_sc[hd, :, dh:2 * dh] = jnp.where(picks_bias, 1.0, 0.0).astype(BF16)
    m_sc[...] = jnp.full(m_sc.shape, NEG_INF, F32)
    acc_sc[...] = jnp.zeros(acc_sc.shape, F32)

    def step(ks, width, row0, masked):
        rows = slice(row0, tq)
        nr = tq - row0
        f_tile = f_all[pl.ds(ks, width), :]
        ones = jnp.ones((width, dh), BF16)
        if masked:
            row = lax.broadcasted_iota(jnp.int32, (nr, width), 0)
            col = lax.broadcasted_iota(jnp.int32, (nr, width), 1)
            causal = col <= row

        def logits(hd):
            k_aug = jnp.concatenate([k_all[pl.ds(ks, width), hd * dh:(hd + 1) * dh], f_tile], axis=1)
            return _dot_nt(q_sc[hd, rows, :], k_aug)

        def weights(hd, s):
            if masked:
                s = jnp.where(causal, s, NEG_INF)
            m_prev = m_sc[hd, rows, :]
            m_new = jnp.maximum(m_prev, jnp.broadcast_to(jnp.max(s, axis=-1, keepdims=True), (nr, dh)))
            m_sc[hd, rows, :] = m_new
            alpha = jnp.exp2(m_prev - m_new)
            p = jnp.exp2(s - jnp.concatenate([m_new] * (width // dh), axis=1)).astype(BF16)
            v_aug = jnp.concatenate([v_all[pl.ds(ks, width), hd * dh:(hd + 1) * dh], ones], axis=1)
            return alpha, _dot(p, v_aug)

        def accumulate(hd, alpha, pv):
            acc_sc[hd, rows, :] = jnp.concatenate([alpha, alpha], axis=1) * acc_sc[hd, rows, :] + pv

        stage1, stage2 = {}, {}
        for n in range(heads + 2):
            if n < heads:
                stage1[n] = logits(n)
            if 1 <= n <= heads:
                stage2[n - 1] = weights(n - 1, stage1.pop(n - 1))
            if n >= 2:
                accumulate(n - 2, *stage2.pop(n - 2))

    def body(j, carry):
        step(pl.multiple_of(j * tq, tq), tq, 0, False)
        return carry

    lax.fori_loop(0, qi, body, 0)
    for r in range(tq // ATT_DIAG):
        step(pl.multiple_of(q0 + r * ATT_DIAG, ATT_DIAG), ATT_DIAG, r * ATT_DIAG, True)

    for hd in range(heads):
        cs = slice(hd * dh, (hd + 1) * dh)
        acc = acc_sc[hd]
        y_sc[:, cs] = (acc[:, 0:dh] / acc[:, dh:2 * dh] * gate_sc[:, cs]).astype(BF16)

    o_ref[0] = x + mod[2:3] * _dot(y_sc[...], wout_ref[...])


def _fox_call(x, mod, w_q, q_norm_g, w_out, k, v, fbias):
    bsz, seq, d = x.shape
    w = w_out.shape[0]
    heads = w // HEAD_DIM
    tq = ATT_TQ
    kern = functools.partial(_fox_kernel, tq=tq, heads=heads)
    return pl.pallas_call(
        kern,
        grid=(bsz, seq // tq),
        in_specs=[
            pl.BlockSpec((1, tq, d), lambda b, s: (b, s, 0)),
            pl.BlockSpec((1, 6, d), lambda b, s: (b, 0, 0)),
            _resident((d, 2 * w), lambda b, s: (0, 0)),
            _resident((1, HEAD_DIM), lambda b, s: (0, 0)),
            pl.BlockSpec((1, tq, w), lambda b, s: (b, s, 0)),
            pl.BlockSpec((1, tq, w), lambda b, s: (b, s, 0)),
            pl.BlockSpec((1, tq, HEAD_DIM), lambda b, s: (b, s, 0)),
            _resident((w, d), lambda b, s: (0, 0)),
        ],
        out_specs=pl.BlockSpec((1, tq, d), lambda b, s: (b, s, 0)),
        out_shape=jax.ShapeDtypeStruct(x.shape, F32),
        scratch_shapes=[
            pltpu.VMEM((heads, tq, 2 * HEAD_DIM), BF16),
            pltpu.VMEM((tq, w), F32),
            pltpu.VMEM((tq, w), BF16),
            pltpu.VMEM((heads, tq, HEAD_DIM), F32),
            pltpu.VMEM((heads, tq, 2 * HEAD_DIM), F32),
            pltpu.VMEM((seq, w), BF16),
            pltpu.VMEM((seq, w), BF16),
            pltpu.VMEM((seq, HEAD_DIM), BF16),
        ],
        compiler_params=_params(("arbitrary", "arbitrary")),
        name="fox_layer",
    )(x, mod, w_q.astype(BF16), q_norm_g.reshape(1, HEAD_DIM), k, v, fbias, w_out.astype(BF16))


def kernel(x, c, ada_w, ada_b, a_w_in, a_lb_logits, a_norm_g, a_w_out, kv_ada_w, kv_ada_b, kv_w, kv_b_f, k_norm_g,
           b_w_q, q_norm_g, b_w_out, ffn_w_up, ffn_conv_w, ffn_conv_b, ffn_w_down):
    bsz, seq, d = x.shape
    depth = ada_w.shape[0]
    n_a = a_w_in.shape[0]
    ada_b3 = ada_b.reshape(depth, 1, 6 * d)
    k_sh = v_sh = fbias = None
    for l in range(depth):
        mod = _mod_call(c, ada_w, ada_b3, l).reshape(bsz, 6, d)
        if l == n_a:
            kv_mod = _mod_call(c, kv_ada_w[None], kv_ada_b.reshape(1, 1, 2 * d), 0).reshape(bsz, 2, d)
            k_sh, v_sh, fbias = _kv_call(x, kv_mod, kv_w, kv_b_f, k_norm_g)
        if l < n_a:
            x = _hgrn_call(x, mod, a_w_in[l], a_w_out[l], a_lb_logits, l, a_norm_g[l])
        else:
            j = l - n_a
            x = _fox_call(x, mod, b_w_q[j], q_norm_g[j], b_w_out[j], k_sh, v_sh, fbias)
        x = _ffn_call(x, mod, ffn_w_up[l], ffn_conv_w[l], ffn_conv_b[l], ffn_w_down[l])
    return x
```

```python
import functools

import jax
import jax.numpy as jnp
from jax import lax
from jax.experimental import pallas as pl
from jax.experimental.pallas import tpu as pltpu

F32 = jnp.float32
BF16 = jnp.bfloat16

EPS = 1e-6
NEG_INF = -1e30
HEAD_DIM = 128
HGRN_CHUNK = 64
CONV_W = 3
LOG2E = 1.4426950408889634
BIAS_LANES = 16

V7X_VMEM_BYTES = 64 * 1024 * 1024
VMEM_LIMIT_BYTES = V7X_VMEM_BYTES - 8 * 1024 * 1024

MOD_TN = 1024
HGRN_TS = 512
HGRN_BLK = 256
FFN_TM = 1024
FFN_FC = 256
ATT_TQ = 512
ATT_DIAG = 256


def _dot(a, b):
    return jnp.dot(a, b, preferred_element_type=F32)


def _dot_nt(a, b):
    return lax.dot_general(a, b, (((1,), (1,)), ((), ())), preferred_element_type=F32)


def _dot_tn(a, b):
    return lax.dot_general(a, b, (((0,), (0,)), ((), ())), preferred_element_type=F32)


def _sigmoid(x):
    return 1.0 / (1.0 + jnp.exp(-x))


def _rms(x):
    return x * lax.rsqrt(jnp.mean(x * x, axis=-1, keepdims=True) + EPS)


def _modulate(x, shift, scale):
    return _rms(x) * (1.0 + scale) + shift


def _split_bf16(x):
    hi = x.astype(BF16)
    lo = (x - hi.astype(F32)).astype(BF16)
    return hi, lo


def _resident(block_shape, index_map):
    return pl.BlockSpec(block_shape, index_map, pipeline_mode=pl.Buffered(1))


def _params(semantics):
    return pltpu.CompilerParams(dimension_semantics=semantics, vmem_limit_bytes=VMEM_LIMIT_BYTES)


def _mod_kernel(c_ref, w_ref, b_ref, o_ref):
    c = c_ref[...]
    ca_hi, ca_lo = _split_bf16(c * _sigmoid(c))
    w_hi, w_lo = _split_bf16(w_ref[...])
    acc = _dot(ca_hi, w_hi) + _dot(ca_lo, w_hi) + _dot(ca_hi, w_lo)
    o_ref[...] = acc + b_ref[...]


def _mod_call(c, w3, b3, layer):
    bsz, d = c.shape
    n = w3.shape[-1]
    tn = MOD_TN
    return pl.pallas_call(
        _mod_kernel,
        grid=(n // tn,),
        in_specs=[
            pl.BlockSpec((bsz, d), lambda j: (0, 0)),
            pl.BlockSpec((None, d, tn), lambda j: (layer, 0, j)),
            pl.BlockSpec((None, 1, tn), lambda j: (layer, 0, j)),
        ],
        out_specs=pl.BlockSpec((bsz, tn), lambda j: (0, j)),
        out_shape=jax.ShapeDtypeStruct((bsz, n), F32),
        compiler_params=_params(("arbitrary",)),
        name="adaln_mod",
    )(c, w3, b3)


def _hgrn_kernel(x_ref, mod_ref, win_ref, wout_ref, lb_ref, ng_ref, tri_ref, o_ref,
                 proj_sc, bc_sc, y_sc, st_sc, *, ts, heads, layer):
    dk = HEAD_DIM
    w = heads * dk
    chunk = HGRN_CHUNK
    blk = HGRN_BLK
    nch = blk // chunk

    @pl.when(pl.program_id(1) == 0)
    def _():
        st_sc[...] = jnp.zeros_like(st_sc)

    x = x_ref[0]
    mod = mod_ref[0]
    h = _modulate(x, mod[0:1], mod[1:2]).astype(BF16)
    proj_sc[...] = _dot(h, win_ref[...])

    lb_e = jnp.exp(lb_ref[...] - jnp.max(lb_ref[...], axis=0, keepdims=True))
    lb = jnp.sum(lb_e[0:layer + 1], axis=0, keepdims=True) / jnp.sum(lb_e, axis=0, keepdims=True)
    fg = lb + (1.0 - lb) * _sigmoid(proj_sc[:, w:2 * w])
    proj_sc[:, w:2 * w] = fg
    logf_hi, logf_lo = _split_bf16(jnp.log(fg))
    tri = tri_ref[...]
    for rb in range(ts // blk):
        rows = slice(rb * blk, (rb + 1) * blk)
        bc_sc[rows] = _dot(tri, logf_hi[rows]) + _dot(tri, logf_lo[rows])

    row = lax.broadcasted_iota(jnp.int32, (blk, blk), 0)
    col = lax.broadcasted_iota(jnp.int32, (blk, blk), 1)
    mask = (row - col).astype(jnp.uint32) <= (row & (chunk - 1)).astype(jnp.uint32)

    def per_chunk(rows_1):
        return jnp.concatenate([jnp.broadcast_to(r, (chunk, dk)) for r in rows_1], axis=0)

    def chunk_rows(c):
        return slice(c * chunk, (c + 1) * chunk)

    def scores_stage(rb, hd):
        rows = slice(rb * blk, (rb + 1) * blk)
        bc = bc_sc[rows, hd * dk:(hd + 1) * dk]
        mid_rows = [bc[c * chunk + chunk // 2:c * chunk + chunk // 2 + 1] for c in range(nch)]
        last_rows = [bc[(c + 1) * chunk - 1:(c + 1) * chunk] for c in range(nch)]
        b_mid = per_chunk(mid_rows)
        q = proj_sc[rows, hd * dk:(hd + 1) * dk]
        q_mid = q * _sigmoid(q) * jnp.exp(bc - b_mid)
        k_mid = (1.0 - proj_sc[rows, w + hd * dk:w + (hd + 1) * dk]) * jnp.exp(b_mid - bc)
        q_inter = (q_mid * per_chunk([jnp.exp(r) for r in mid_rows])).astype(BF16)
        k_state = (k_mid * per_chunk([jnp.exp(l - r) for l, r in zip(last_rows, mid_rows)])).astype(BF16)
        v = proj_sc[rows, 2 * w + hd * dk:2 * w + (hd + 1) * dk].astype(BF16)
        s = _dot_nt(q_mid.astype(BF16), k_mid.astype(BF16))
        incs = [_dot_tn(v[chunk_rows(c)], k_state[chunk_rows(c)]) for c in range(nch)]
        return s, incs, q_inter, v, [jnp.exp(r) for r in last_rows]

    def state_stage(hd, s, incs, q_inter, v, decays):
        o_intra = _dot(jnp.where(mask, s, 0.0).astype(BF16), v)
        states = [st_sc[hd]]
        for c in range(nch):
            states.append(decays[c] * states[c] + incs[c])
        st_sc[hd] = states[nch]
        inter = [_dot_nt(q_inter[chunk_rows(c)], states[c].astype(BF16)) for c in range(nch)]
        return o_intra, inter

    def output_stage(rb, hd, o_intra, inter):
        rows = slice(rb * blk, (rb + 1) * blk)
        cs = slice(hd * dk, (hd + 1) * dk)
        o = jnp.concatenate([o_intra[chunk_rows(c)] + inter[c] for c in range(nch)], axis=0)
        g = proj_sc[rows, 3 * w + hd * dk:3 * w + (hd + 1) * dk]
        y_sc[rows, cs] = (_rms(o) * (g * _sigmoid(g) * ng_ref[:, cs])).astype(BF16)

    items = [(rb, hd) for rb in range(ts // blk) for hd in range(heads)]
    stage1, stage2 = {}, {}
    for n in range(len(items) + 2):
        if n < len(items):
            stage1[n] = scores_stage(*items[n])
        if 1 <= n <= len(items):
            stage2[n - 1] = state_stage(items[n - 1][1], *stage1.pop(n - 1))
        if n >= 2:
            output_stage(*items[n - 2], *stage2.pop(n - 2))

    o_ref[0] = x + mod[2:3] * _dot(y_sc[...], wout_ref[...])


def _hgrn_call(x, mod, w_in, w_out, lb_logits, layer, norm_g):
    bsz, seq, d = x.shape
    ts = HGRN_TS
    heads = w_out.shape[0] // HEAD_DIM
    w = heads * HEAD_DIM
    n_lb = lb_logits.shape[0]
    idx = jnp.arange(HGRN_BLK)
    tri = ((idx[:, None] >= idx[None, :]) & (idx[:, None] // HGRN_CHUNK == idx[None, :] // HGRN_CHUNK)).astype(BF16)
    kern = functools.partial(_hgrn_kernel, ts=ts, heads=heads, layer=layer)
    return pl.pallas_call(
        kern,
        grid=(bsz, seq // ts),
        in_specs=[
            pl.BlockSpec((1, ts, d), lambda b, s: (b, s, 0)),
            pl.BlockSpec((1, 6, d), lambda b, s: (b, 0, 0)),
            _resident((d, 4 * w), lambda b, s: (0, 0)),
            _resident((w, d), lambda b, s: (0, 0)),
            _resident((n_lb, w), lambda b, s: (0, 0)),
            _resident((1, w), lambda b, s: (0, 0)),
            _resident((HGRN_BLK, HGRN_BLK), lambda b, s: (0, 0)),
        ],
        out_specs=pl.BlockSpec((1, ts, d), lambda b, s: (b, s, 0)),
        out_shape=jax.ShapeDtypeStruct(x.shape, F32),
        scratch_shapes=[
            pltpu.VMEM((ts, 4 * w), F32),
            pltpu.VMEM((ts, w), F32),
            pltpu.VMEM((ts, w), BF16),
            pltpu.VMEM((heads, HEAD_DIM, HEAD_DIM), F32),
        ],
        compiler_params=_params(("arbitrary", "arbitrary")),
        name="hgrn2_layer",
    )(x, mod, w_in.astype(BF16), w_out.astype(BF16), lb_logits.astype(F32),
      jnp.tile(norm_g, heads).reshape(1, w), tri)


def _ffn_kernel(x_ref, mod_ref, wup_ref, cw_ref, cb_ref, wdn_ref, o_ref, tail_sc, perm_sc, act_sc, *, tm, ff):
    fc = FFN_FC
    d = x_ref.shape[-1]
    nj = tm // 8
    n_slab = d // 128

    @pl.when(pl.program_id(1) == 0)
    def _():
        tail_sc[...] = jnp.zeros_like(tail_sc)

    def perm_rows(jn):
        g = nj // 8
        return pl.ds((jn % g) * 64 + jn // g, 8, stride=8)

    x = x_ref[0]
    mod = mod_ref[0]
    h_nat = _modulate(x, mod[3:4], mod[4:5])
    for sl in range(n_slab):
        for jn in range(nj):
            perm_sc[sl, perm_rows(jn), :] = h_nat[8 * jn:8 * jn + 8, sl * 128:(sl + 1) * 128]
    h = jnp.concatenate([perm_sc[sl] for sl in range(n_slab)], axis=1).astype(BF16)

    first_row = lax.broadcasted_iota(jnp.int32, (8, fc), 0) == 0

    def conv_cols(c0):
        u = _dot(h, wup_ref[:, c0:c0 + fc])
        prev = tail_sc[:, c0:c0 + fc]
        tail_sc[:, c0:c0 + fc] = u[tm - 16:tm]
        back1 = jnp.where(first_row, pltpu.roll(prev[8:16], 1, 0), pltpu.roll(u[tm - 8:tm], 1, 0))
        back2 = jnp.where(first_row, pltpu.roll(prev[0:8], 1, 0), pltpu.roll(u[tm - 16:tm - 8], 1, 0))
        u1 = jnp.concatenate([back1, u[0:tm - 8]], axis=0)
        u2 = jnp.concatenate([back2, back1, u[0:tm - 16]], axis=0)
        cw = cw_ref[:, c0:c0 + fc]
        return cw[2:3] * u + cw[1:2] * u1 + cw[0:1] * u2 + cb_ref[:, c0:c0 + fc]

    for j in range(ff // fc):
        gate = conv_cols(j * fc)
        val = conv_cols(ff + j * fc)
        act_sc[:, j * fc:(j + 1) * fc] = (gate * _sigmoid(gate) * val).astype(BF16)
    y = _dot(act_sc[...], wdn_ref[...])

    for sl in range(n_slab):
        perm_sc[sl] = y[:, sl * 128:(sl + 1) * 128]
    y_nat = jnp.concatenate(
        [jnp.concatenate([perm_sc[sl, perm_rows(jn), :] for jn in range(nj)], axis=0) for sl in range(n_slab)],
        axis=1)
    o_ref[0] = x + mod[5:6] * y_nat


def _ffn_call(x, mod, w_up, conv_w, conv_b, w_down):
    bsz, seq, d = x.shape
    ff = w_down.shape[0]
    tm = FFN_TM
    kern = functools.partial(_ffn_kernel, tm=tm, ff=ff)
    return pl.pallas_call(
        kern,
        grid=(bsz, seq // tm),
        in_specs=[
            pl.BlockSpec((1, tm, d), lambda b, s: (b, s, 0)),
            pl.BlockSpec((1, 6, d), lambda b, s: (b, 0, 0)),
            _resident((d, 2 * ff), lambda b, s: (0, 0)),
            _resident((CONV_W, 2 * ff), lambda b, s: (0, 0)),
            _resident((1, 2 * ff), lambda b, s: (0, 0)),
            _resident((ff, d), lambda b, s: (0, 0)),
        ],
        out_specs=pl.BlockSpec((1, tm, d), lambda b, s: (b, s, 0)),
        out_shape=jax.ShapeDtypeStruct(x.shape, F32),
        scratch_shapes=[
            pltpu.VMEM((16, 2 * ff), F32),
            pltpu.VMEM((d // 128, tm, 128), F32),
            pltpu.VMEM((tm, ff), BF16),
        ],
        compiler_params=_params(("arbitrary", "arbitrary")),
        name="conv_glu_ffn",
    )(x, mod, w_up.astype(BF16), conv_w, conv_b.reshape(1, 2 * ff), w_down.astype(BF16))


def _fox_kernel(x_ref, mod_ref, kvmod_ref, wkv_ref, wf_ref, bf_ref, kg_ref, tri_ref, wq_ref, qg_ref, wout_ref, o_ref,
                q_sc, gate_sc, y_sc, m_sc, acc_sc, k_all, v_all, f_all, carry_sc, *, tq, heads):
    dh = HEAD_DIM
    w = heads * dh
    qi = pl.program_id(1)
    q0 = pl.multiple_of(qi * tq, tq)
    tile = pl.ds(q0, tq)

    @pl.when(qi == 0)
    def _():
        carry_sc[...] = jnp.zeros_like(carry_sc)

    x = x_ref[0]
    mod = mod_ref[0]
    kvmod = kvmod_ref[0]
    xn = _rms(x)
    hk = (xn * (1.0 + kvmod[1:2]) + kvmod[0:1]).astype(BF16)
    h = (xn * (1.0 + mod[1:2]) + mod[0:1]).astype(BF16)

    kg = kg_ref[...]
    half = w // 2

    def keys(c0):
        kk = _dot(hk, wkv_ref[:, c0:c0 + half])
        for hd in range(half // dh):
            cs = slice(hd * dh, (hd + 1) * dh)
            k_all[tile, c0 + hd * dh:c0 + (hd + 1) * dh] = (_rms(kk[:, cs]) * kg).astype(BF16)

    def values(c0):
        v_all[tile, c0:c0 + half] = _dot(hk, wkv_ref[:, w + c0:w + c0 + half]).astype(BF16)

    z = _dot(hk, wf_ref[...]) + bf_ref[...]
    keys(0)
    log_f = jnp.minimum(z, 0.0) - jnp.log(1.0 + jnp.exp(-jnp.abs(z)))
    hi, lo = _split_bf16(log_f)
    tri = tri_ref[...]
    cum = _dot(tri, hi) + _dot(tri, lo) + carry_sc[0:1, :]
    keys(half)
    carry_sc[...] = jnp.broadcast_to(cum[tq - 1:tq, :], carry_sc.shape)
    values(0)
    t0 = -LOG2E * cum
    b0 = t0.astype(BF16).astype(F32)
    t1 = t0 - b0
    b1 = t1.astype(BF16).astype(F32)
    b2 = t1 - b1
    sub = lax.broadcasted_iota(jnp.int32, (tq, dh), 1) & (BIAS_LANES - 1)
    f_all[tile, :] = jnp.where(sub == 0, b0, jnp.where(sub == 1, b1, jnp.where(sub == 2, b2, 0.0))).astype(BF16)
    values(half)

    pq = _dot(h, wq_ref[...])
    gate_sc[...] = _sigmoid(pq[:, w:2 * w])
    qg = qg_ref[...] * (dh ** -0.5 * LOG2E)
    lane = lax.broadcasted_iota(jnp.int32, (tq, dh), 1)
    for hd in range(heads):
        cs = slice(hd * dh, (hd + 1) * dh)
        q_sc[hd, :, 0:dh] = (_rms(pq[:, cs]) * qg).astype(BF16)
        picks_bias = (lane - hd * BIAS_LANES).astype(jnp.uint32) < 3
        q_sc[hd, :, dh:2 * dh] = jnp.where(picks_bias, 1.0, 0.0).astype(BF16)
    m_sc[...] = jnp.full(m_sc.shape, NEG_INF, F32)
    acc_sc[...] = jnp.zeros(acc_sc.shape, F32)

    def run_tiles(tiles):
        ctx = []
        for ks, width, row0, masked in tiles:
            nr = tq - row0
            causal = None
            if masked:
                row = lax.broadcasted_iota(jnp.int32, (nr, width), 0)
                col = lax.broadcasted_iota(jnp.int32, (nr, width), 1)
                causal = col <= row
            ctx.append((ks, width, slice(row0, tq), nr, causal, f_all[pl.ds(ks, width), :], jnp.ones((width, dh), BF16)))

        def logits(t, hd):
            ks, width, rows, _, _, f_tile, _ = ctx[t]
            k_aug = jnp.concatenate([k_all[pl.ds(ks, width), hd * dh:(hd + 1) * dh], f_tile], axis=1)
            return _dot_nt(q_sc[hd, rows, :], k_aug)

        def weights(t, hd, s):
            ks, width, rows, nr, causal, _, ones = ctx[t]
            if causal is not None:
                s = jnp.where(causal, s, NEG_INF)
            m_prev = m_sc[hd, rows, :]
            m_new = jnp.maximum(m_prev, jnp.broadcast_to(jnp.max(s, axis=-1, keepdims=True), (nr, dh)))
            m_sc[hd, rows, :] = m_new
            alpha = jnp.exp2(m_prev - m_new)
            p = jnp.exp2(s - jnp.concatenate([m_new] * (width // dh), axis=1)).astype(BF16)
            v_aug = jnp.concatenate([v_all[pl.ds(ks, width), hd * dh:(hd + 1) * dh], ones], axis=1)
            return alpha, _dot(p, v_aug)

        def accumulate(t, hd, alpha, pv):
            rows = ctx[t][2]
            acc_sc[hd, rows, :] = jnp.concatenate([alpha, alpha], axis=1) * acc_sc[hd, rows, :] + pv

        items = [(t, hd) for t in range(len(tiles)) for hd in range(heads)]
        stage1, stage2 = {}, {}
        for n in range(len(items) + 2):
            if n < len(items):
                stage1[n] = logits(*items[n])
            if 1 <= n <= len(items):
                stage2[n - 1] = weights(*items[n - 1], stage1.pop(n - 1))
            if n >= 2:
                accumulate(*items[n - 2], *stage2.pop(n - 2))

    def full_tile(j):
        return (pl.multiple_of(j * tq, tq), tq, 0, False)

    def pair_body(jj, carry):
        run_tiles([full_tile(2 * jj), full_tile(2 * jj + 1)])
        return carry

    lax.fori_loop(0, qi // 2, pair_body, 0)

    @pl.when(qi % 2 == 1)
    def _():
        run_tiles([full_tile(qi - 1)])

    run_tiles([(pl.multiple_of(q0 + r * ATT_DIAG, ATT_DIAG), ATT_DIAG, r * ATT_DIAG, True)
               for r in range(tq // ATT_DIAG)])

    for hd in range(heads):
        cs = slice(hd * dh, (hd + 1) * dh)
        acc = acc_sc[hd]
        y_sc[:, cs] = (acc[:, 0:dh] / acc[:, dh:2 * dh] * gate_sc[:, cs]).astype(BF16)

    o_ref[0] = x + mod[2:3] * _dot(y_sc[...], wout_ref[...])


def _fox_call(x, mod, kv_mod, kv_w, kv_b_f, k_norm_g, w_q, q_norm_g, w_out):
    bsz, seq, d = x.shape
    w = w_out.shape[0]
    heads = w // HEAD_DIM
    tq = ATT_TQ
    assert heads * BIAS_LANES <= HEAD_DIM
    idx = jnp.arange(tq)
    tri = (idx[:, None] >= idx[None, :]).astype(BF16)
    lane = jnp.arange(HEAD_DIM)
    head_of_lane = jnp.minimum(lane // BIAS_LANES, heads - 1)
    used = (lane // BIAS_LANES < heads) & (lane % BIAS_LANES < 3)
    wf = jnp.where(used[None, :], kv_w[:, 2 * w:][:, head_of_lane], 0.0).astype(BF16)
    bf = jnp.where(used, kv_b_f.astype(F32)[head_of_lane], 0.0).reshape(1, HEAD_DIM)
    kern = functools.partial(_fox_kernel, tq=tq, heads=heads)
    return pl.pallas_call(
        kern,
        grid=(bsz, seq // tq),
        in_specs=[
            pl.BlockSpec((1, tq, d), lambda b, s: (b, s, 0)),
            pl.BlockSpec((1, 6, d), lambda b, s: (b, 0, 0)),
            pl.BlockSpec((1, 2, d), lambda b, s: (b, 0, 0)),
            _resident((d, 2 * w), lambda b, s: (0, 0)),
            _resident((d, HEAD_DIM), lambda b, s: (0, 0)),
            _resident((1, HEAD_DIM), lambda b, s: (0, 0)),
            _resident((1, HEAD_DIM), lambda b, s: (0, 0)),
            _resident((tq, tq), lambda b, s: (0, 0)),
            _resident((d, 2 * w), lambda b, s: (0, 0)),
            _resident((1, HEAD_DIM), lambda b, s: (0, 0)),
            _resident((w, d), lambda b, s: (0, 0)),
        ],
        out_specs=pl.BlockSpec((1, tq, d), lambda b, s: (b, s, 0)),
        out_shape=jax.ShapeDtypeStruct(x.shape, F32),
        scratch_shapes=[
            pltpu.VMEM((heads, tq, 2 * HEAD_DIM), BF16),
            pltpu.VMEM((tq, w), F32),
            pltpu.VMEM((tq, w), BF16),
            pltpu.VMEM((heads, tq, HEAD_DIM), F32),
            pltpu.VMEM((heads, tq, 2 * HEAD_DIM), F32),
            pltpu.VMEM((seq, w), BF16),
            pltpu.VMEM((seq, w), BF16),
            pltpu.VMEM((seq, HEAD_DIM), BF16),
            pltpu.VMEM((8, HEAD_DIM), F32),
        ],
        compiler_params=_params(("arbitrary", "arbitrary")),
        name="fox_layer",
    )(x, mod, kv_mod, kv_w[:, :2 * w].astype(BF16), wf, bf, k_norm_g.reshape(1, HEAD_DIM), tri,
      w_q.astype(BF16), q_norm_g.reshape(1, HEAD_DIM), w_out.astype(BF16))


def kernel(x, c, ada_w, ada_b, a_w_in, a_lb_logits, a_norm_g, a_w_out, kv_ada_w, kv_ada_b, kv_w, kv_b_f, k_norm_g,
           b_w_q, q_norm_g, b_w_out, ffn_w_up, ffn_conv_w, ffn_conv_b, ffn_w_down):
    bsz, seq, d = x.shape
    depth = ada_w.shape[0]
    n_a = a_w_in.shape[0]
    ada_b3 = ada_b.reshape(depth, 1, 6 * d)
    assert depth - n_a == 1, "fox_layer computes the shared K/V in-kernel: exactly one FoX layer is supported"
    for l in range(depth):
        mod = _mod_call(c, ada_w, ada_b3, l).reshape(bsz, 6, d)
        if l < n_a:
            x = _hgrn_call(x, mod, a_w_in[l], a_w_out[l], a_lb_logits, l, a_norm_g[l])
        else:
            j = l - n_a
            kv_mod = _mod_call(c, kv_ada_w[None], kv_ada_b.reshape(1, 1, 2 * d), 0).reshape(bsz, 2, d)
            x = _fox_call(x, mod, kv_mod, kv_w, kv_b_f, k_norm_g, b_w_q[j], q_norm_g[j], b_w_out[j])
        x = _ffn_call(x, mod, ffn_w_up[l], ffn_conv_w[l], ffn_conv_b[l], ffn_w_down[l])
    return x
```

```python
import functools

import jax
import jax.numpy as jnp
from jax import lax
from jax.experimental import pallas as pl
from jax.experimental.pallas import tpu as pltpu

F32 = jnp.float32
BF16 = jnp.bfloat16

EPS = 1e-6
NEG_INF = -1e30
HEAD_DIM = 128
HGRN_CHUNK = 64
CONV_W = 3
LOG2E = 1.4426950408889634
BIAS_LANES = 16

V7X_VMEM_BYTES = 64 * 1024 * 1024
VMEM_LIMIT_BYTES = V7X_VMEM_BYTES - 8 * 1024 * 1024

MOD_TN = 1024
HGRN_TS = 512
HGRN_BLK = 256
FFN_TM = 1024
FFN_FC = 256
ATT_TQ = 512
ATT_DIAG = 256


def _dot(a, b):
    return jnp.dot(a, b, preferred_element_type=F32)


def _dot_nt(a, b):
    return lax.dot_general(a, b, (((1,), (1,)), ((), ())), preferred_element_type=F32)


def _dot_tn(a, b):
    return lax.dot_general(a, b, (((0,), (0,)), ((), ())), preferred_element_type=F32)


def _sigmoid(x):
    return 1.0 / (1.0 + jnp.exp(-x))


def _rms(x):
    return x * lax.rsqrt(jnp.mean(x * x, axis=-1, keepdims=True) + EPS)


def _modulate(x, shift, scale):
    return _rms(x) * (1.0 + scale) + shift


def _split_bf16(x):
    hi = x.astype(BF16)
    lo = (x - hi.astype(F32)).astype(BF16)
    return hi, lo


def _resident(block_shape, index_map):
    return pl.BlockSpec(block_shape, index_map, pipeline_mode=pl.Buffered(1))


def _params(semantics):
    return pltpu.CompilerParams(dimension_semantics=semantics, vmem_limit_bytes=VMEM_LIMIT_BYTES)


def _mod_kernel(c_ref, w_ref, b_ref, o_ref):
    c = c_ref[...]
    ca_hi, ca_lo = _split_bf16(c * _sigmoid(c))
    w_hi, w_lo = _split_bf16(w_ref[...])
    acc = _dot(ca_hi, w_hi) + _dot(ca_lo, w_hi) + _dot(ca_hi, w_lo)
    o_ref[...] = acc + b_ref[...]


def _mod_call(c, w3, b3, layer):
    bsz, d = c.shape
    n = w3.shape[-1]
    tn = MOD_TN
    return pl.pallas_call(
        _mod_kernel,
        grid=(n // tn,),
        in_specs=[
            pl.BlockSpec((bsz, d), lambda j: (0, 0)),
            pl.BlockSpec((None, d, tn), lambda j: (layer, 0, j)),
            pl.BlockSpec((None, 1, tn), lambda j: (layer, 0, j)),
        ],
        out_specs=pl.BlockSpec((bsz, tn), lambda j: (0, j)),
        out_shape=jax.ShapeDtypeStruct((bsz, n), F32),
        compiler_params=_params(("arbitrary",)),
        name="adaln_mod",
    )(c, w3, b3)


def _hgrn_kernel(x_ref, mod_ref, win_ref, wout_ref, lb_ref, ng_ref, tri_ref, o_ref,
                 proj_sc, bc_sc, y_sc, st_sc, *, ts, heads, layer):
    dk = HEAD_DIM
    w = heads * dk
    chunk = HGRN_CHUNK
    blk = HGRN_BLK
    nch = blk // chunk

    @pl.when(pl.program_id(1) == 0)
    def _():
        st_sc[...] = jnp.zeros_like(st_sc)

    x = x_ref[0]
    mod = mod_ref[0]
    h = _modulate(x, mod[0:1], mod[1:2]).astype(BF16)
    proj_sc[...] = _dot(h, win_ref[...])

    lb_e = jnp.exp(lb_ref[...] - jnp.max(lb_ref[...], axis=0, keepdims=True))
    lb = jnp.sum(lb_e[0:layer + 1], axis=0, keepdims=True) / jnp.sum(lb_e, axis=0, keepdims=True)
    fg = lb + (1.0 - lb) * _sigmoid(proj_sc[:, w:2 * w])
    proj_sc[:, w:2 * w] = fg
    logf_hi, logf_lo = _split_bf16(jnp.log(fg))
    tri = tri_ref[...]
    for rb in range(ts // blk):
        rows = slice(rb * blk, (rb + 1) * blk)
        bc_sc[rows] = _dot(tri, logf_hi[rows]) + _dot(tri, logf_lo[rows])

    row = lax.broadcasted_iota(jnp.int32, (blk, blk), 0)
    col = lax.broadcasted_iota(jnp.int32, (blk, blk), 1)
    mask = (row - col).astype(jnp.uint32) <= (row & (chunk - 1)).astype(jnp.uint32)

    def per_chunk(rows_1):
        return jnp.concatenate([jnp.broadcast_to(r, (chunk, dk)) for r in rows_1], axis=0)

    def chunk_rows(c):
        return slice(c * chunk, (c + 1) * chunk)

    def scores_stage(rb, hd):
        rows = slice(rb * blk, (rb + 1) * blk)
        bc = bc_sc[rows, hd * dk:(hd + 1) * dk]
        mid_rows = [bc[c * chunk + chunk // 2:c * chunk + chunk // 2 + 1] for c in range(nch)]
        last_rows = [bc[(c + 1) * chunk - 1:(c + 1) * chunk] for c in range(nch)]
        b_mid = per_chunk(mid_rows)
        q = proj_sc[rows, hd * dk:(hd + 1) * dk]
        q_mid = q * _sigmoid(q) * jnp.exp(bc - b_mid)
        k_mid = (1.0 - proj_sc[rows, w + hd * dk:w + (hd + 1) * dk]) * jnp.exp(b_mid - bc)
        q_inter = (q_mid * per_chunk([jnp.exp(r) for r in mid_rows])).astype(BF16)
        k_state = (k_mid * per_chunk([jnp.exp(l - r) for l, r in zip(last_rows, mid_rows)])).astype(BF16)
        v = proj_sc[rows, 2 * w + hd * dk:2 * w + (hd + 1) * dk].astype(BF16)
        s = _dot_nt(q_mid.astype(BF16), k_mid.astype(BF16))
        incs = [_dot_tn(v[chunk_rows(c)], k_state[chunk_rows(c)]) for c in range(nch)]
        return s, incs, q_inter, v, [jnp.exp(r) for r in last_rows]

    def state_stage(hd, s, incs, q_inter, v, decays):
        o_intra = _dot(jnp.where(mask, s, 0.0).astype(BF16), v)
        states = [st_sc[hd]]
        for c in range(nch):
            states.append(decays[c] * states[c] + incs[c])
        st_sc[hd] = states[nch]
        inter = [_dot_nt(q_inter[chunk_rows(c)], states[c].astype(BF16)) for c in range(nch)]
        return o_intra, inter

    def output_stage(rb, hd, o_intra, inter):
        rows = slice(rb * blk, (rb + 1) * blk)
        cs = slice(hd * dk, (hd + 1) * dk)
        o = jnp.concatenate([o_intra[chunk_rows(c)] + inter[c] for c in range(nch)], axis=0)
        g = proj_sc[rows, 3 * w + hd * dk:3 * w + (hd + 1) * dk]
        y_sc[rows, cs] = (_rms(o) * (g * _sigmoid(g) * ng_ref[:, cs])).astype(BF16)

    items = [(rb, hd) for rb in range(ts // blk) for hd in range(heads)]
    stage1, stage2 = {}, {}
    for n in range(len(items) + 2):
        if n < len(items):
            stage1[n] = scores_stage(*items[n])
        if 1 <= n <= len(items):
            stage2[n - 1] = state_stage(items[n - 1][1], *stage1.pop(n - 1))
        if n >= 2:
            output_stage(*items[n - 2], *stage2.pop(n - 2))

    o_ref[0] = x + mod[2:3] * _dot(y_sc[...], wout_ref[...])


def _hgrn_call(x, mod, w_in, w_out, lb_logits, layer, norm_g):
    bsz, seq, d = x.shape
    ts = HGRN_TS
    heads = w_out.shape[0] // HEAD_DIM
    w = heads * HEAD_DIM
    n_lb = lb_logits.shape[0]
    idx = jnp.arange(HGRN_BLK)
    tri = ((idx[:, None] >= idx[None, :]) & (idx[:, None] // HGRN_CHUNK == idx[None, :] // HGRN_CHUNK)).astype(BF16)
    kern = functools.partial(_hgrn_kernel, ts=ts, heads=heads, layer=layer)
    return pl.pallas_call(
        kern,
        grid=(bsz, seq // ts),
        in_specs=[
            pl.BlockSpec((1, ts, d), lambda b, s: (b, s, 0)),
            pl.BlockSpec((1, 6, d), lambda b, s: (b, 0, 0)),
            _resident((d, 4 * w), lambda b, s: (0, 0)),
            _resident((w, d), lambda b, s: (0, 0)),
            _resident((n_lb, w), lambda b, s: (0, 0)),
            _resident((1, w), lambda b, s: (0, 0)),
            _resident((HGRN_BLK, HGRN_BLK), lambda b, s: (0, 0)),
        ],
        out_specs=pl.BlockSpec((1, ts, d), lambda b, s: (b, s, 0)),
        out_shape=jax.ShapeDtypeStruct(x.shape, F32),
        scratch_shapes=[
            pltpu.VMEM((ts, 4 * w), F32),
            pltpu.VMEM((ts, w), F32),
            pltpu.VMEM((ts, w), BF16),
            pltpu.VMEM((heads, HEAD_DIM, HEAD_DIM), F32),
        ],
        compiler_params=_params(("arbitrary", "arbitrary")),
        name="hgrn2_layer",
    )(x, mod, w_in.astype(BF16), w_out.astype(BF16), lb_logits.astype(F32),
      jnp.tile(norm_g, heads).reshape(1, w), tri)


def _ffn_kernel(x_ref, mod_ref, wup_ref, cw_ref, cb_ref, wdn_ref, o_ref, tail_sc, perm_sc, act_sc, *, tm, ff):
    fc = FFN_FC
    d = x_ref.shape[-1]
    nj = tm // 8
    n_slab = d // 128

    @pl.when(pl.program_id(1) == 0)
    def _():
        tail_sc[...] = jnp.zeros_like(tail_sc)

    def perm_rows(jn):
        g = nj // 8
        return pl.ds((jn % g) * 64 + jn // g, 8, stride=8)

    x = x_ref[0]
    mod = mod_ref[0]
    h_nat = _modulate(x, mod[3:4], mod[4:5])
    for sl in range(n_slab):
        for jn in range(nj):
            perm_sc[sl, perm_rows(jn), :] = h_nat[8 * jn:8 * jn + 8, sl * 128:(sl + 1) * 128]
    h = jnp.concatenate([perm_sc[sl] for sl in range(n_slab)], axis=1).astype(BF16)

    first_row = lax.broadcasted_iota(jnp.int32, (8, fc), 0) == 0

    def conv_cols(c0):
        u = _dot(h, wup_ref[:, c0:c0 + fc])
        prev = tail_sc[:, c0:c0 + fc]
        tail_sc[:, c0:c0 + fc] = u[tm - 16:tm]
        back1 = jnp.where(first_row, pltpu.roll(prev[8:16], 1, 0), pltpu.roll(u[tm - 8:tm], 1, 0))
        back2 = jnp.where(first_row, pltpu.roll(prev[0:8], 1, 0), pltpu.roll(u[tm - 16:tm - 8], 1, 0))
        u1 = jnp.concatenate([back1, u[0:tm - 8]], axis=0)
        u2 = jnp.concatenate([back2, back1, u[0:tm - 16]], axis=0)
        cw = cw_ref[:, c0:c0 + fc]
        return cw[2:3] * u + cw[1:2] * u1 + cw[0:1] * u2 + cb_ref[:, c0:c0 + fc]

    for j in range(ff // fc):
        gate = conv_cols(j * fc)
        val = conv_cols(ff + j * fc)
        act_sc[:, j * fc:(j + 1) * fc] = (gate * _sigmoid(gate) * val).astype(BF16)
    y = _dot(act_sc[...], wdn_ref[...])

    for sl in range(n_slab):
        perm_sc[sl] = y[:, sl * 128:(sl + 1) * 128]
    y_nat = jnp.concatenate(
        [jnp.concatenate([perm_sc[sl, perm_rows(jn), :] for jn in range(nj)], axis=0) for sl in range(n_slab)],
        axis=1)
    o_ref[0] = x + mod[5:6] * y_nat


def _ffn_call(x, mod, w_up, conv_w, conv_b, w_down, layer):
    bsz, seq, d = x.shape
    ff = w_down.shape[1]
    tm = FFN_TM
    kern = functools.partial(_ffn_kernel, tm=tm, ff=ff)
    return pl.pallas_call(
        kern,
        grid=(bsz, seq // tm),
        in_specs=[
            pl.BlockSpec((1, tm, d), lambda b, s: (b, s, 0)),
            pl.BlockSpec((1, 6, d), lambda b, s: (b, 0, 0)),
            _resident((None, d, 2 * ff), lambda b, s: (layer, 0, 0)),
            _resident((None, CONV_W, 2 * ff), lambda b, s: (layer, 0, 0)),
            _resident((None, 1, 2 * ff), lambda b, s: (layer, 0, 0)),
            _resident((None, ff, d), lambda b, s: (layer, 0, 0)),
        ],
        out_specs=pl.BlockSpec((1, tm, d), lambda b, s: (b, s, 0)),
        out_shape=jax.ShapeDtypeStruct(x.shape, F32),
        scratch_shapes=[
            pltpu.VMEM((16, 2 * ff), F32),
            pltpu.VMEM((d // 128, tm, 128), F32),
            pltpu.VMEM((tm, ff), BF16),
        ],
        compiler_params=_params(("arbitrary", "arbitrary")),
        name="conv_glu_ffn",
    )(x, mod, w_up, conv_w, conv_b, w_down)


def _fox_kernel(x_ref, mod_ref, kvmod_ref, wkv_ref, wf_ref, bf_ref, kg_ref, tri_ref, wq_ref, qg_ref, wout_ref, o_ref,
                q_sc, gate_sc, y_sc, m_sc, acc_sc, k_all, v_all, f_all, carry_sc, *, tq, heads):
    dh = HEAD_DIM
    w = heads * dh
    qi = pl.program_id(1)
    q0 = pl.multiple_of(qi * tq, tq)
    tile = pl.ds(q0, tq)

    @pl.when(qi == 0)
    def _():
        carry_sc[...] = jnp.zeros_like(carry_sc)

    x = x_ref[0]
    mod = mod_ref[0]
    kvmod = kvmod_ref[0]
    xn = _rms(x)
    hk = (xn * (1.0 + kvmod[1:2]) + kvmod[0:1]).astype(BF16)
    h = (xn * (1.0 + mod[1:2]) + mod[0:1]).astype(BF16)

    kg = kg_ref[...]
    half = w // 2

    def keys(c0):
        kk = _dot(hk, wkv_ref[:, c0:c0 + half])
        for hd in range(half // dh):
            cs = slice(hd * dh, (hd + 1) * dh)
            k_all[tile, c0 + hd * dh:c0 + (hd + 1) * dh] = (_rms(kk[:, cs]) * kg).astype(BF16)

    def values(c0):
        v_all[tile, c0:c0 + half] = _dot(hk, wkv_ref[:, w + c0:w + c0 + half]).astype(BF16)

    z = _dot(hk, wf_ref[...]) + bf_ref[...]
    keys(0)
    log_f = jnp.minimum(z, 0.0) - jnp.log(1.0 + jnp.exp(-jnp.abs(z)))
    hi, lo = _split_bf16(log_f)
    tri = tri_ref[...]
    cum = _dot(tri, hi) + _dot(tri, lo) + carry_sc[0:1, :]
    keys(half)
    carry_sc[...] = jnp.broadcast_to(cum[tq - 1:tq, :], carry_sc.shape)
    values(0)
    t0 = -LOG2E * cum
    b0 = t0.astype(BF16).astype(F32)
    t1 = t0 - b0
    b1 = t1.astype(BF16).astype(F32)
    b2 = t1 - b1
    sub = lax.broadcasted_iota(jnp.int32, (tq, dh), 1) & (BIAS_LANES - 1)
    f_all[tile, :] = jnp.where(sub == 0, b0, jnp.where(sub == 1, b1, jnp.where(sub == 2, b2, 0.0))).astype(BF16)
    values(half)

    pq = _dot(h, wq_ref[...])
    gate_sc[...] = _sigmoid(pq[:, w:2 * w])
    qg = qg_ref[...] * (dh ** -0.5 * LOG2E)
    lane = lax.broadcasted_iota(jnp.int32, (tq, dh), 1)
    for hd in range(heads):
        cs = slice(hd * dh, (hd + 1) * dh)
        q_sc[hd, :, 0:dh] = (_rms(pq[:, cs]) * qg).astype(BF16)
        picks_bias = (lane - hd * BIAS_LANES).astype(jnp.uint32) < 3
        q_sc[hd, :, dh:2 * dh] = jnp.where(picks_bias, 1.0, 0.0).astype(BF16)
    m_sc[...] = jnp.full(m_sc.shape, NEG_INF, F32)
    acc_sc[...] = jnp.zeros(acc_sc.shape, F32)

    def run_tiles(tiles):
        ctx = []
        for ks, width, row0, masked in tiles:
            nr = tq - row0
            causal = None
            if masked:
                row = lax.broadcasted_iota(jnp.int32, (nr, width), 0)
                col = lax.broadcasted_iota(jnp.int32, (nr, width), 1)
                causal = col <= row
            ctx.append((ks, width, slice(row0, tq), nr, causal, f_all[pl.ds(ks, width), :], jnp.ones((width, dh), BF16)))

        def logits(t, hd):
            ks, width, rows, _, _, f_tile, _ = ctx[t]
            k_aug = jnp.concatenate([k_all[pl.ds(ks, width), hd * dh:(hd + 1) * dh], f_tile], axis=1)
            return _dot_nt(q_sc[hd, rows, :], k_aug)

        def weights(t, hd, s):
            ks, width, rows, nr, causal, _, ones = ctx[t]
            if causal is not None:
                s = jnp.where(causal, s, NEG_INF)
            m_prev = m_sc[hd, rows, :]
            m_new = jnp.maximum(m_prev, jnp.broadcast_to(jnp.max(s, axis=-1, keepdims=True), (nr, dh)))
            m_sc[hd, rows, :] = m_new
            alpha = jnp.exp2(m_prev - m_new)
            p = jnp.exp2(s - jnp.concatenate([m_new] * (width // dh), axis=1)).astype(BF16)
            v_aug = jnp.concatenate([v_all[pl.ds(ks, width), hd * dh:(hd + 1) * dh], ones], axis=1)
            return alpha, _dot(p, v_aug)

        def accumulate(t, hd, alpha, pv):
            rows = ctx[t][2]
            acc_sc[hd, rows, :] = jnp.concatenate([alpha, alpha], axis=1) * acc_sc[hd, rows, :] + pv

        items = [(t, hd) for t in range(len(tiles)) for hd in range(heads)]
        stage1, stage2 = {}, {}
        for n in range(len(items) + 2):
            if n < len(items):
                stage1[n] = logits(*items[n])
            if 1 <= n <= len(items):
                stage2[n - 1] = weights(*items[n - 1], stage1.pop(n - 1))
            if n >= 2:
                accumulate(*items[n - 2], *stage2.pop(n - 2))

    def full_tile(j):
        return (pl.multiple_of(j * tq, tq), tq, 0, False)

    def pair_body(jj, carry):
        run_tiles([full_tile(2 * jj), full_tile(2 * jj + 1)])
        return carry

    lax.fori_loop(0, qi // 2, pair_body, 0)

    @pl.when(qi % 2 == 1)
    def _():
        run_tiles([full_tile(qi - 1)])

    run_tiles([(pl.multiple_of(q0 + r * ATT_DIAG, ATT_DIAG), ATT_DIAG, r * ATT_DIAG, True)
               for r in range(tq // ATT_DIAG)])

    for hd in range(heads):
        cs = slice(hd * dh, (hd + 1) * dh)
        acc = acc_sc[hd]
        y_sc[:, cs] = (acc[:, 0:dh] / acc[:, dh:2 * dh] * gate_sc[:, cs]).astype(BF16)

    o_ref[0] = x + mod[2:3] * _dot(y_sc[...], wout_ref[...])


def _fox_call(x, mod, kv_mod, kv_w, kv_b_f, k_norm_g, w_q, q_norm_g, w_out):
    bsz, seq, d = x.shape
    w = w_out.shape[0]
    heads = w // HEAD_DIM
    tq = ATT_TQ
    assert heads * BIAS_LANES <= HEAD_DIM
    idx = jnp.arange(tq)
    tri = (idx[:, None] >= idx[None, :]).astype(BF16)
    lane = jnp.arange(HEAD_DIM)
    head_of_lane = jnp.minimum(lane // BIAS_LANES, heads - 1)
    used = (lane // BIAS_LANES < heads) & (lane % BIAS_LANES < 3)
    wf = jnp.where(used[None, :], kv_w[:, 2 * w:][:, head_of_lane], 0.0).astype(BF16)
    bf = jnp.where(used, kv_b_f.astype(F32)[head_of_lane], 0.0).reshape(1, HEAD_DIM)
    kern = functools.partial(_fox_kernel, tq=tq, heads=heads)
    return pl.pallas_call(
        kern,
        grid=(bsz, seq // tq),
        in_specs=[
            pl.BlockSpec((1, tq, d), lambda b, s: (b, s, 0)),
            pl.BlockSpec((1, 6, d), lambda b, s: (b, 0, 0)),
            pl.BlockSpec((1, 2, d), lambda b, s: (b, 0, 0)),
            _resident((d, 2 * w), lambda b, s: (0, 0)),
            _resident((d, HEAD_DIM), lambda b, s: (0, 0)),
            _resident((1, HEAD_DIM), lambda b, s: (0, 0)),
            _resident((1, HEAD_DIM), lambda b, s: (0, 0)),
            _resident((tq, tq), lambda b, s: (0, 0)),
            _resident((d, 2 * w), lambda b, s: (0, 0)),
            _resident((1, HEAD_DIM), lambda b, s: (0, 0)),
            _resident((w, d), lambda b, s: (0, 0)),
        ],
        out_specs=pl.BlockSpec((1, tq, d), lambda b, s: (b, s, 0)),
        out_shape=jax.ShapeDtypeStruct(x.shape, F32),
        scratch_shapes=[
            pltpu.VMEM((heads, tq, 2 * HEAD_DIM), BF16),
            pltpu.VMEM((tq, w), F32),
            pltpu.VMEM((tq, w), BF16),
            pltpu.VMEM((heads, tq, HEAD_DIM), F32),
            pltpu.VMEM((heads, tq, 2 * HEAD_DIM), F32),
            pltpu.VMEM((seq, w), BF16),
            pltpu.VMEM((seq, w), BF16),
            pltpu.VMEM((seq, HEAD_DIM), BF16),
            pltpu.VMEM((8, HEAD_DIM), F32),
        ],
        compiler_params=_params(("arbitrary", "arbitrary")),
        name="fox_layer",
    )(x, mod, kv_mod, kv_w[:, :2 * w].astype(BF16), wf, bf, k_norm_g.reshape(1, HEAD_DIM), tri,
      w_q.astype(BF16), q_norm_g.reshape(1, HEAD_DIM), w_out.astype(BF16))


def kernel(x, c, ada_w, ada_b, a_w_in, a_lb_logits, a_norm_g, a_w_out, kv_ada_w, kv_ada_b, kv_w, kv_b_f, k_norm_g,
           b_w_q, q_norm_g, b_w_out, ffn_w_up, ffn_conv_w, ffn_conv_b, ffn_w_down):
    bsz, seq, d = x.shape
    depth = ada_w.shape[0]
    n_a = a_w_in.shape[0]
    ada_b3 = ada_b.reshape(depth, 1, 6 * d)
    assert depth - n_a == 1, "fox_layer computes the shared K/V in-kernel: exactly one FoX layer is supported"
    ffn_up = ffn_w_up.astype(BF16)
    ffn_down = ffn_w_down.astype(BF16)
    ffn_cb = ffn_conv_b.reshape(depth, 1, ffn_conv_b.shape[-1])
    for l in range(depth):
        mod = _mod_call(c, ada_w, ada_b3, l).reshape(bsz, 6, d)
        if l < n_a:
            x = _hgrn_call(x, mod, a_w_in[l], a_w_out[l], a_lb_logits, l, a_norm_g[l])
        else:
            j = l - n_a
            kv_mod = _mod_call(c, kv_ada_w[None], kv_ada_b.reshape(1, 1, 2 * d), 0).reshape(bsz, 2, d)
            x = _fox_call(x, mod, kv_mod, kv_w, kv_b_f, k_norm_g, b_w_q[j], q_norm_g[j], b_w_out[j])
        x = _ffn_call(x, mod, ffn_up, ffn_conv_w, ffn_cb, ffn_down, l)
    return x
```

```python
import functools

import jax
import jax.numpy as jnp
from jax import lax
from jax.experimental import pallas as pl
from jax.experimental.pallas import tpu as pltpu

F32 = jnp.float32
BF16 = jnp.bfloat16

EPS = 1e-6
NEG_INF = -1e30
HEAD_DIM = 128
HGRN_CHUNK = 64
CONV_W = 3
LOG2E = 1.4426950408889634
BIAS_LANES = 16

V7X_VMEM_BYTES = 64 * 1024 * 1024
VMEM_LIMIT_BYTES = V7X_VMEM_BYTES - 8 * 1024 * 1024
SUBLANES = 8
LANES = 128

MOD_TN = 1024
HGRN_TS = 512
HGRN_BLK = 256
FFN_TM = 1024
FFN_FC = 256
ATT_TQ = 512
ATT_DIAG = 256


def _dot(a, b):
    return jnp.dot(a, b, preferred_element_type=F32)


def _dot_nt(a, b):
    return lax.dot_general(a, b, (((1,), (1,)), ((), ())), preferred_element_type=F32)


def _dot_tn(a, b):
    return lax.dot_general(a, b, (((0,), (0,)), ((), ())), preferred_element_type=F32)


def _sigmoid(x):
    return 1.0 / (1.0 + jnp.exp(-x))


def _rms(x):
    return x * lax.rsqrt(jnp.mean(x * x, axis=-1, keepdims=True) + EPS)


def _modulate(x, shift, scale):
    return _rms(x) * (1.0 + scale) + shift


def _split_bf16(x):
    hi = x.astype(BF16)
    lo = (x - hi.astype(F32)).astype(BF16)
    return hi, lo


def _resident(block_shape, index_map):
    return pl.BlockSpec(block_shape, index_map, pipeline_mode=pl.Buffered(1))


def _params(semantics):
    return pltpu.CompilerParams(dimension_semantics=semantics, vmem_limit_bytes=VMEM_LIMIT_BYTES)


def _mod_kernel(c_ref, w_ref, b_ref, o_ref):
    c = c_ref[...]
    ca_hi, ca_lo = _split_bf16(c * _sigmoid(c))
    w_hi, w_lo = _split_bf16(w_ref[...])
    acc = _dot(ca_hi, w_hi) + _dot(ca_lo, w_hi) + _dot(ca_hi, w_lo)
    o_ref[...] = acc + b_ref[...]


def _mod_call(c, w3, b3, layer):
    bsz, d = c.shape
    n = w3.shape[-1]
    tn = MOD_TN
    return pl.pallas_call(
        _mod_kernel,
        grid=(n // tn,),
        in_specs=[
            pl.BlockSpec((bsz, d), lambda j: (0, 0)),
            pl.BlockSpec((None, d, tn), lambda j: (layer, 0, j)),
            pl.BlockSpec((None, 1, tn), lambda j: (layer, 0, j)),
        ],
        out_specs=pl.BlockSpec((bsz, tn), lambda j: (0, j)),
        out_shape=jax.ShapeDtypeStruct((bsz, n), F32),
        compiler_params=_params(("arbitrary",)),
        name="adaln_mod",
    )(c, w3, b3)


def _hgrn_kernel(x_ref, mod_ref, win_ref, wout_ref, lb_ref, ng_ref, tri_ref, o_ref,
                 proj_sc, bc_sc, y_sc, st_sc, *, ts, heads, layer):
    dk = HEAD_DIM
    w = heads * dk
    chunk = HGRN_CHUNK
    blk = HGRN_BLK
    nch = blk // chunk

    @pl.when(pl.program_id(1) == 0)
    def _():
        st_sc[...] = jnp.zeros_like(st_sc)

    x = x_ref[0]
    mod = mod_ref[0]
    h = _modulate(x, mod[0:1], mod[1:2]).astype(BF16)
    proj_sc[...] = _dot(h, win_ref[...])

    lb_e = jnp.exp(lb_ref[...] - jnp.max(lb_ref[...], axis=0, keepdims=True))
    lb = jnp.sum(lb_e[0:layer + 1], axis=0, keepdims=True) / jnp.sum(lb_e, axis=0, keepdims=True)
    fg = lb + (1.0 - lb) * _sigmoid(proj_sc[:, w:2 * w])
    proj_sc[:, w:2 * w] = fg
    logf_hi, logf_lo = _split_bf16(jnp.log(fg))
    tri = tri_ref[...]
    for rb in range(ts // blk):
        rows = slice(rb * blk, (rb + 1) * blk)
        bc_sc[rows] = _dot(tri, logf_hi[rows]) + _dot(tri, logf_lo[rows])

    row = lax.broadcasted_iota(jnp.int32, (blk, blk), 0)
    col = lax.broadcasted_iota(jnp.int32, (blk, blk), 1)
    mask = (row - col).astype(jnp.uint32) <= (row & (chunk - 1)).astype(jnp.uint32)

    def per_chunk(rows_1):
        return jnp.concatenate([jnp.broadcast_to(r, (chunk, dk)) for r in rows_1], axis=0)

    def chunk_rows(c):
        return slice(c * chunk, (c + 1) * chunk)

    def scores_stage(rb, hd):
        rows = slice(rb * blk, (rb + 1) * blk)
        bc = bc_sc[rows, hd * dk:(hd + 1) * dk]
        mid_rows = [bc[c * chunk + chunk // 2:c * chunk + chunk // 2 + 1] for c in range(nch)]
        last_rows = [bc[(c + 1) * chunk - 1:(c + 1) * chunk] for c in range(nch)]
        b_mid = per_chunk(mid_rows)
        q = proj_sc[rows, hd * dk:(hd + 1) * dk]
        q_mid = q * _sigmoid(q) * jnp.exp(bc - b_mid)
        k_mid = (1.0 - proj_sc[rows, w + hd * dk:w + (hd + 1) * dk]) * jnp.exp(b_mid - bc)
        q_inter = (q_mid * per_chunk([jnp.exp(r) for r in mid_rows])).astype(BF16)
        k_state = (k_mid * per_chunk([jnp.exp(l - r) for l, r in zip(last_rows, mid_rows)])).astype(BF16)
        v = proj_sc[rows, 2 * w + hd * dk:2 * w + (hd + 1) * dk].astype(BF16)
        s = _dot_nt(q_mid.astype(BF16), k_mid.astype(BF16))
        incs = [_dot_tn(v[chunk_rows(c)], k_state[chunk_rows(c)]) for c in range(nch)]
        return s, incs, q_inter, v, [jnp.exp(r) for r in last_rows]

    def state_stage(hd, s, incs, q_inter, v, decays):
        o_intra = _dot(jnp.where(mask, s, 0.0).astype(BF16), v)
        states = [st_sc[hd]]
        for c in range(nch):
            states.append(decays[c] * states[c] + incs[c])
        st_sc[hd] = states[nch]
        inter = [_dot_nt(q_inter[chunk_rows(c)], states[c].astype(BF16)) for c in range(nch)]
        return o_intra, inter

    def output_stage(rb, hd, o_intra, inter):
        rows = slice(rb * blk, (rb + 1) * blk)
        cs = slice(hd * dk, (hd + 1) * dk)
        o = jnp.concatenate([o_intra[chunk_rows(c)] + inter[c] for c in range(nch)], axis=0)
        g = proj_sc[rows, 3 * w + hd * dk:3 * w + (hd + 1) * dk]
        y_sc[rows, cs] = (_rms(o) * (g * _sigmoid(g) * ng_ref[:, cs])).astype(BF16)

    items = [(rb, hd) for rb in range(ts // blk) for hd in range(heads)]
    stage1, stage2 = {}, {}
    for n in range(len(items) + 2):
        if n < len(items):
            stage1[n] = scores_stage(*items[n])
        if 1 <= n <= len(items):
            stage2[n - 1] = state_stage(items[n - 1][1], *stage1.pop(n - 1))
        if n >= 2:
            output_stage(*items[n - 2], *stage2.pop(n - 2))

    o_ref[0] = x + mod[2:3] * _dot(y_sc[...], wout_ref[...])


def _hgrn_call(x, mod, w_in, w_out, lb_logits, layer, norm_g):
    bsz, seq, d = x.shape
    ts = HGRN_TS
    heads = w_out.shape[0] // HEAD_DIM
    w = heads * HEAD_DIM
    n_lb = lb_logits.shape[0]
    idx = jnp.arange(HGRN_BLK)
    tri = ((idx[:, None] >= idx[None, :]) & (idx[:, None] // HGRN_CHUNK == idx[None, :] // HGRN_CHUNK)).astype(BF16)
    kern = functools.partial(_hgrn_kernel, ts=ts, heads=heads, layer=layer)
    return pl.pallas_call(
        kern,
        grid=(bsz, seq // ts),
        in_specs=[
            pl.BlockSpec((1, ts, d), lambda b, s: (b, s, 0)),
            pl.BlockSpec((1, 6, d), lambda b, s: (b, 0, 0)),
            _resident((d, 4 * w), lambda b, s: (0, 0)),
            _resident((w, d), lambda b, s: (0, 0)),
            _resident((n_lb, w), lambda b, s: (0, 0)),
            _resident((1, w), lambda b, s: (0, 0)),
            _resident((HGRN_BLK, HGRN_BLK), lambda b, s: (0, 0)),
        ],
        out_specs=pl.BlockSpec((1, ts, d), lambda b, s: (b, s, 0)),
        out_shape=jax.ShapeDtypeStruct(x.shape, F32),
        scratch_shapes=[
            pltpu.VMEM((ts, 4 * w), F32),
            pltpu.VMEM((ts, w), F32),
            pltpu.VMEM((ts, w), BF16),
            pltpu.VMEM((heads, HEAD_DIM, HEAD_DIM), F32),
        ],
        compiler_params=_params(("arbitrary", "arbitrary")),
        name="hgrn2_layer",
    )(x, mod, w_in.astype(BF16), w_out.astype(BF16), lb_logits.astype(F32),
      jnp.tile(norm_g, heads).reshape(1, w), tri)


def _ffn_kernel(x_ref, mod_ref, wup_ref, cw_ref, cb_ref, wdn_ref, o_ref, tail_sc, perm_sc, act_sc, *, tm, ff):
    fc = FFN_FC
    r = SUBLANES
    d = x_ref.shape[-1]
    nj = tm // r
    n_slab = d // LANES

    @pl.when(pl.program_id(1) == 0)
    def _():
        tail_sc[...] = jnp.zeros_like(tail_sc)

    def perm_rows(jn):
        g = nj // r
        return pl.ds((jn % g) * r * r + jn // g, r, stride=r)

    x = x_ref[0]
    mod = mod_ref[0]
    h_nat = _modulate(x, mod[3:4], mod[4:5])
    for sl in range(n_slab):
        for jn in range(nj):
            perm_sc[sl, perm_rows(jn), :] = h_nat[r * jn:r * jn + r, sl * LANES:(sl + 1) * LANES]
    h = jnp.concatenate([perm_sc[sl] for sl in range(n_slab)], axis=1).astype(BF16)

    first_row = lax.broadcasted_iota(jnp.int32, (r, fc), 0) == 0

    def conv_cols(c0):
        u = _dot(h, wup_ref[:, c0:c0 + fc])
        prev = tail_sc[:, c0:c0 + fc]
        tail_sc[:, c0:c0 + fc] = u[tm - 2 * r:tm]
        back1 = jnp.where(first_row, pltpu.roll(prev[r:2 * r], 1, 0), pltpu.roll(u[tm - r:tm], 1, 0))
        back2 = jnp.where(first_row, pltpu.roll(prev[0:r], 1, 0), pltpu.roll(u[tm - 2 * r:tm - r], 1, 0))
        u1 = jnp.concatenate([back1, u[0:tm - r]], axis=0)
        u2 = jnp.concatenate([back2, back1, u[0:tm - 2 * r]], axis=0)
        cw = cw_ref[:, c0:c0 + fc]
        return cw[2:3] * u + cw[1:2] * u1 + cw[0:1] * u2 + cb_ref[:, c0:c0 + fc]

    for j in range(ff // fc):
        gate = conv_cols(j * fc)
        val = conv_cols(ff + j * fc)
        act_sc[:, j * fc:(j + 1) * fc] = (gate * _sigmoid(gate) * val).astype(BF16)
    y = _dot(act_sc[...], wdn_ref[...])

    for sl in range(n_slab):
        perm_sc[sl] = y[:, sl * LANES:(sl + 1) * LANES]
    y_nat = jnp.concatenate(
        [jnp.concatenate([perm_sc[sl, perm_rows(jn), :] for jn in range(nj)], axis=0) for sl in range(n_slab)],
        axis=1)
    o_ref[0] = x + mod[5:6] * y_nat


def _ffn_call(x, mod, w_up, conv_w, conv_b, w_down, layer):
    bsz, seq, d = x.shape
    ff = w_down.shape[1]
    tm = FFN_TM
    kern = functools.partial(_ffn_kernel, tm=tm, ff=ff)
    return pl.pallas_call(
        kern,
        grid=(bsz, seq // tm),
        in_specs=[
            pl.BlockSpec((1, tm, d), lambda b, s: (b, s, 0)),
            pl.BlockSpec((1, 6, d), lambda b, s: (b, 0, 0)),
            _resident((None, d, 2 * ff), lambda b, s: (layer, 0, 0)),
            _resident((None, CONV_W, 2 * ff), lambda b, s: (layer, 0, 0)),
            _resident((None, 1, 2 * ff), lambda b, s: (layer, 0, 0)),
            _resident((None, ff, d), lambda b, s: (layer, 0, 0)),
        ],
        out_specs=pl.BlockSpec((1, tm, d), lambda b, s: (b, s, 0)),
        out_shape=jax.ShapeDtypeStruct(x.shape, F32),
        scratch_shapes=[
            pltpu.VMEM((2 * SUBLANES, 2 * ff), F32),
            pltpu.VMEM((d // LANES, tm, LANES), F32),
            pltpu.VMEM((tm, ff), BF16),
        ],
        compiler_params=_params(("arbitrary", "arbitrary")),
        name="conv_glu_ffn",
    )(x, mod, w_up, conv_w, conv_b, w_down)


def _fox_kernel(x_ref, mod_ref, kvmod_ref, wkv_ref, wf_ref, bf_ref, kg_ref, tri_ref, wq_ref, qg_ref, wout_ref, o_ref,
                q_sc, gate_sc, y_sc, m_sc, acc_sc, k_all, v_all, f_all, carry_sc, *, tq, heads):
    dh = HEAD_DIM
    w = heads * dh
    qi = pl.program_id(1)
    q0 = pl.multiple_of(qi * tq, tq)
    tile = pl.ds(q0, tq)

    @pl.when(qi == 0)
    def _():
        carry_sc[...] = jnp.zeros_like(carry_sc)

    x = x_ref[0]
    mod = mod_ref[0]
    kvmod = kvmod_ref[0]
    xn = _rms(x)
    hk = (xn * (1.0 + kvmod[1:2]) + kvmod[0:1]).astype(BF16)
    h = (xn * (1.0 + mod[1:2]) + mod[0:1]).astype(BF16)

    kg = kg_ref[...]
    half = w // 2

    def keys(c0):
        kk = _dot(hk, wkv_ref[:, c0:c0 + half])
        for hd in range(half // dh):
            cs = slice(hd * dh, (hd + 1) * dh)
            k_all[tile, c0 + hd * dh:c0 + (hd + 1) * dh] = (_rms(kk[:, cs]) * kg).astype(BF16)

    def values(c0):
        v_all[tile, c0:c0 + half] = _dot(hk, wkv_ref[:, w + c0:w + c0 + half]).astype(BF16)

    z = _dot(hk, wf_ref[...]) + bf_ref[...]
    keys(0)
    log_f = jnp.minimum(z, 0.0) - jnp.log(1.0 + jnp.exp(-jnp.abs(z)))
    hi, lo = _split_bf16(log_f)
    tri = tri_ref[...]
    cum = _dot(tri, hi) + _dot(tri, lo) + carry_sc[0:1, :]
    keys(half)
    carry_sc[...] = jnp.broadcast_to(cum[tq - 1:tq, :], carry_sc.shape)
    values(0)
    t0 = -LOG2E * cum
    b0 = t0.astype(BF16).astype(F32)
    t1 = t0 - b0
    b1 = t1.astype(BF16).astype(F32)
    b2 = t1 - b1
    sub = lax.broadcasted_iota(jnp.int32, (tq, dh), 1) & (BIAS_LANES - 1)
    f_all[tile, :] = jnp.where(sub == 0, b0, jnp.where(sub == 1, b1, jnp.where(sub == 2, b2, 0.0))).astype(BF16)
    values(half)

    pq = _dot(h, wq_ref[...])
    gate_sc[...] = _sigmoid(pq[:, w:2 * w])
    qg = qg_ref[...] * (dh ** -0.5 * LOG2E)
    lane = lax.broadcasted_iota(jnp.int32, (tq, dh), 1)
    for hd in range(heads):
        cs = slice(hd * dh, (hd + 1) * dh)
        q_sc[hd, :, 0:dh] = (_rms(pq[:, cs]) * qg).astype(BF16)
        picks_bias = (lane - hd * BIAS_LANES).astype(jnp.uint32) < 3
        q_sc[hd, :, dh:2 * dh] = jnp.where(picks_bias, 1.0, 0.0).astype(BF16)
    m_sc[...] = jnp.full(m_sc.shape, NEG_INF, F32)
    acc_sc[...] = jnp.zeros(acc_sc.shape, F32)

    def run_tiles(tiles):
        ctx = []
        for ks, width, row0, masked in tiles:
            nr = tq - row0
            causal = None
            if masked:
                row = lax.broadcasted_iota(jnp.int32, (nr, width), 0)
                col = lax.broadcasted_iota(jnp.int32, (nr, width), 1)
                causal = col <= row
            ctx.append((ks, width, slice(row0, tq), nr, causal, f_all[pl.ds(ks, width), :], jnp.ones((width, dh), BF16)))

        def logits(t, hd):
            ks, width, rows, _, _, f_tile, _ = ctx[t]
            k_aug = jnp.concatenate([k_all[pl.ds(ks, width), hd * dh:(hd + 1) * dh], f_tile], axis=1)
            return _dot_nt(q_sc[hd, rows, :], k_aug)

        def weights(t, hd, s):
            ks, width, rows, nr, causal, _, ones = ctx[t]
            if causal is not None:
                s = jnp.where(causal, s, NEG_INF)
            m_prev = m_sc[hd, rows, :]
            m_new = jnp.maximum(m_prev, jnp.broadcast_to(jnp.max(s, axis=-1, keepdims=True), (nr, dh)))
            m_sc[hd, rows, :] = m_new
            alpha = jnp.exp2(m_prev - m_new)
            p = jnp.exp2(s - jnp.concatenate([m_new] * (width // dh), axis=1)).astype(BF16)
            v_aug = jnp.concatenate([v_all[pl.ds(ks, width), hd * dh:(hd + 1) * dh], ones], axis=1)
            return alpha, _dot(p, v_aug)

        def accumulate(t, hd, alpha, pv):
            rows = ctx[t][2]
            acc_sc[hd, rows, :] = jnp.concatenate([alpha, alpha], axis=1) * acc_sc[hd, rows, :] + pv

        items = [(t, hd) for t in range(len(tiles)) for hd in range(heads)]
        stage1, stage2 = {}, {}
        for n in range(len(items) + 2):
            if n < len(items):
                stage1[n] = logits(*items[n])
            if 1 <= n <= len(items):
                stage2[n - 1] = weights(*items[n - 1], stage1.pop(n - 1))
            if n >= 2:
                accumulate(*items[n - 2], *stage2.pop(n - 2))

    def full_tile(j):
        return (pl.multiple_of(j * tq, tq), tq, 0, False)

    def pair_body(jj, carry):
        run_tiles([full_tile(2 * jj), full_tile(2 * jj + 1)])
        return carry

    lax.fori_loop(0, qi // 2, pair_body, 0)

    @pl.when(qi % 2 == 1)
    def _():
        run_tiles([full_tile(qi - 1)])

    run_tiles([(pl.multiple_of(q0 + r * ATT_DIAG, ATT_DIAG), ATT_DIAG, r * ATT_DIAG, True)
               for r in range(tq // ATT_DIAG)])

    for hd in range(heads):
        cs = slice(hd * dh, (hd + 1) * dh)
        acc = acc_sc[hd]
        y_sc[:, cs] = (acc[:, 0:dh] / acc[:, dh:2 * dh] * gate_sc[:, cs]).astype(BF16)

    o_ref[0] = x + mod[2:3] * _dot(y_sc[...], wout_ref[...])


def _fox_call(x, mod, kv_mod, kv_w, kv_b_f, k_norm_g, w_q, q_norm_g, w_out):
    bsz, seq, d = x.shape
    w = w_out.shape[0]
    heads = w // HEAD_DIM
    tq = ATT_TQ
    assert heads * BIAS_LANES <= HEAD_DIM
    idx = jnp.arange(tq)
    tri = (idx[:, None] >= idx[None, :]).astype(BF16)
    lane = jnp.arange(HEAD_DIM)
    head_of_lane = jnp.minimum(lane // BIAS_LANES, heads - 1)
    used = (lane // BIAS_LANES < heads) & (lane % BIAS_LANES < 3)
    wf = jnp.where(used[None, :], kv_w[:, 2 * w:][:, head_of_lane], 0.0).astype(BF16)
    bf = jnp.where(used, kv_b_f.astype(F32)[head_of_lane], 0.0).reshape(1, HEAD_DIM)
    kern = functools.partial(_fox_kernel, tq=tq, heads=heads)
    return pl.pallas_call(
        kern,
        grid=(bsz, seq // tq),
        in_specs=[
            pl.BlockSpec((1, tq, d), lambda b, s: (b, s, 0)),
            pl.BlockSpec((1, 6, d), lambda b, s: (b, 0, 0)),
            pl.BlockSpec((1, 2, d), lambda b, s: (b, 0, 0)),
            _resident((d, 2 * w), lambda b, s: (0, 0)),
            _resident((d, HEAD_DIM), lambda b, s: (0, 0)),
            _resident((1, HEAD_DIM), lambda b, s: (0, 0)),
            _resident((1, HEAD_DIM), lambda b, s: (0, 0)),
            _resident((tq, tq), lambda b, s: (0, 0)),
            _resident((d, 2 * w), lambda b, s: (0, 0)),
            _resident((1, HEAD_DIM), lambda b, s: (0, 0)),
            _resident((w, d), lambda b, s: (0, 0)),
        ],
        out_specs=pl.BlockSpec((1, tq, d), lambda b, s: (b, s, 0)),
        out_shape=jax.ShapeDtypeStruct(x.shape, F32),
        scratch_shapes=[
            pltpu.VMEM((heads, tq, 2 * HEAD_DIM), BF16),
            pltpu.VMEM((tq, w), F32),
            pltpu.VMEM((tq, w), BF16),
            pltpu.VMEM((heads, tq, HEAD_DIM), F32),
            pltpu.VMEM((heads, tq, 2 * HEAD_DIM), F32),
            pltpu.VMEM((seq, w), BF16),
            pltpu.VMEM((seq, w), BF16),
            pltpu.VMEM((seq, HEAD_DIM), BF16),
            pltpu.VMEM((SUBLANES, HEAD_DIM), F32),
        ],
        compiler_params=_params(("arbitrary", "arbitrary")),
        name="fox_layer",
    )(x, mod, kv_mod, kv_w[:, :2 * w].astype(BF16), wf, bf, k_norm_g.reshape(1, HEAD_DIM), tri,
      w_q.astype(BF16), q_norm_g.reshape(1, HEAD_DIM), w_out.astype(BF16))


def kernel(x, c, ada_w, ada_b, a_w_in, a_lb_logits, a_norm_g, a_w_out, kv_ada_w, kv_ada_b, kv_w, kv_b_f, k_norm_g,
           b_w_q, q_norm_g, b_w_out, ffn_w_up, ffn_conv_w, ffn_conv_b, ffn_w_down):
    bsz, seq, d = x.shape
    depth = ada_w.shape[0]
    n_a = a_w_in.shape[0]
    ada_b3 = ada_b.reshape(depth, 1, 6 * d)
    assert depth - n_a == 1, "fox_layer computes the shared K/V in-kernel: exactly one FoX layer is supported"
    ffn_up = ffn_w_up.astype(BF16)
    ffn_down = ffn_w_down.astype(BF16)
    ffn_cb = ffn_conv_b.reshape(depth, 1, ffn_conv_b.shape[-1])
    for l in range(depth):
        mod = _mod_call(c, ada_w, ada_b3, l).reshape(bsz, 6, d)
        if l < n_a:
            x = _hgrn_call(x, mod, a_w_in[l], a_w_out[l], a_lb_logits, l, a_norm_g[l])
        else:
            j = l - n_a
            kv_mod = _mod_call(c, kv_ada_w[None], kv_ada_b.reshape(1, 1, 2 * d), 0).reshape(bsz, 2, d)
            x = _fox_call(x, mod, kv_mod, kv_w, kv_b_f, k_norm_g, b_w_q[j], q_norm_g[j], b_w_out[j])
        x = _ffn_call(x, mod, ffn_up, ffn_conv_w, ffn_cb, ffn_down, l)
    return x
```

```python
import functools

import jax
import jax.numpy as jnp
from jax import lax
from jax.experimental import pallas as pl
from jax.experimental.pallas import tpu as pltpu

F32 = jnp.float32
BF16 = jnp.bfloat16

EPS = 1e-6
NEG_INF = -1e30
HEAD_DIM = 128
HGRN_CHUNK = 64
CONV_W = 3
LOG2E = 1.4426950408889634
BIAS_LANES = 16

V7X_VMEM_BYTES = 64 * 1024 * 1024
VMEM_LIMIT_BYTES = V7X_VMEM_BYTES - 8 * 1024 * 1024
SUBLANES = 8
LANES = 128

MOD_TN = 1024
HGRN_TS = 1024
HGRN_BLK = 256
FFN_TM = 1024
FFN_FC = 256
ATT_TQ = 512
ATT_DIAG = 256


def _dot(a, b):
    return jnp.dot(a, b, preferred_element_type=F32)


def _dot_nt(a, b):
    return lax.dot_general(a, b, (((1,), (1,)), ((), ())), preferred_element_type=F32)


def _dot_tn(a, b):
    return lax.dot_general(a, b, (((0,), (0,)), ((), ())), preferred_element_type=F32)


def _sigmoid(x):
    return 1.0 / (1.0 + jnp.exp(-x))


def _rms(x):
    return x * lax.rsqrt(jnp.mean(x * x, axis=-1, keepdims=True) + EPS)


def _modulate(x, shift, scale):
    return _rms(x) * (1.0 + scale) + shift


def _split_bf16(x):
    hi = x.astype(BF16)
    lo = (x - hi.astype(F32)).astype(BF16)
    return hi, lo


def _resident(block_shape, index_map):
    return pl.BlockSpec(block_shape, index_map, pipeline_mode=pl.Buffered(1))


def _params(semantics):
    return pltpu.CompilerParams(dimension_semantics=semantics, vmem_limit_bytes=VMEM_LIMIT_BYTES)


def _mod_kernel(c_ref, w_ref, b_ref, o_ref):
    c = c_ref[...]
    ca_hi, ca_lo = _split_bf16(c * _sigmoid(c))
    w_hi, w_lo = _split_bf16(w_ref[...])
    acc = _dot(ca_hi, w_hi) + _dot(ca_lo, w_hi) + _dot(ca_hi, w_lo)
    o_ref[...] = acc + b_ref[...]


def _mod_call(c, w3, b3, layer):
    bsz, d = c.shape
    n = w3.shape[-1]
    tn = MOD_TN
    return pl.pallas_call(
        _mod_kernel,
        grid=(n // tn,),
        in_specs=[
            pl.BlockSpec((bsz, d), lambda j: (0, 0)),
            pl.BlockSpec((None, d, tn), lambda j: (layer, 0, j)),
            pl.BlockSpec((None, 1, tn), lambda j: (layer, 0, j)),
        ],
        out_specs=pl.BlockSpec((bsz, tn), lambda j: (0, j)),
        out_shape=jax.ShapeDtypeStruct((bsz, n), F32),
        compiler_params=_params(("arbitrary",)),
        name="adaln_mod",
    )(c, w3, b3)


def _hgrn_kernel(x_ref, mod_ref, win_ref, wout_ref, lb_ref, ng_ref, tri_ref, o_ref,
                 proj_sc, bc_sc, y_sc, st_sc, *, ts, heads, layer):
    dk = HEAD_DIM
    w = heads * dk
    chunk = HGRN_CHUNK
    blk = HGRN_BLK
    nch = blk // chunk

    @pl.when(pl.program_id(1) == 0)
    def _():
        st_sc[...] = jnp.zeros_like(st_sc)

    x = x_ref[0]
    mod = mod_ref[0]
    h = _modulate(x, mod[0:1], mod[1:2]).astype(BF16)
    proj_sc[...] = _dot(h, win_ref[...])

    lb_e = jnp.exp(lb_ref[...] - jnp.max(lb_ref[...], axis=0, keepdims=True))
    lb = jnp.sum(lb_e[0:layer + 1], axis=0, keepdims=True) / jnp.sum(lb_e, axis=0, keepdims=True)
    fg = lb + (1.0 - lb) * _sigmoid(proj_sc[:, w:2 * w])
    proj_sc[:, w:2 * w] = fg
    logf_hi, logf_lo = _split_bf16(jnp.log(fg))
    tri = tri_ref[...]
    for rb in range(ts // blk):
        rows = slice(rb * blk, (rb + 1) * blk)
        bc_sc[rows] = _dot(tri, logf_hi[rows]) + _dot(tri, logf_lo[rows])

    row = lax.broadcasted_iota(jnp.int32, (blk, blk), 0)
    col = lax.broadcasted_iota(jnp.int32, (blk, blk), 1)
    mask = (row - col).astype(jnp.uint32) <= (row & (chunk - 1)).astype(jnp.uint32)

    def per_chunk(rows_1):
        return jnp.concatenate([jnp.broadcast_to(r, (chunk, dk)) for r in rows_1], axis=0)

    def chunk_rows(c):
        return slice(c * chunk, (c + 1) * chunk)

    def scores_stage(rb, hd):
        rows = slice(rb * blk, (rb + 1) * blk)
        bc = bc_sc[rows, hd * dk:(hd + 1) * dk]
        mid_rows = [bc[c * chunk + chunk // 2:c * chunk + chunk // 2 + 1] for c in range(nch)]
        last_rows = [bc[(c + 1) * chunk - 1:(c + 1) * chunk] for c in range(nch)]
        b_mid = per_chunk(mid_rows)
        q = proj_sc[rows, hd * dk:(hd + 1) * dk]
        q_mid = q * _sigmoid(q) * jnp.exp(bc - b_mid)
        k_mid = (1.0 - proj_sc[rows, w + hd * dk:w + (hd + 1) * dk]) * jnp.exp(b_mid - bc)
        q_inter = (q_mid * per_chunk([jnp.exp(r) for r in mid_rows])).astype(BF16)
        k_state = (k_mid * per_chunk([jnp.exp(l - r) for l, r in zip(last_rows, mid_rows)])).astype(BF16)
        v = proj_sc[rows, 2 * w + hd * dk:2 * w + (hd + 1) * dk].astype(BF16)
        s = _dot_nt(q_mid.astype(BF16), k_mid.astype(BF16))
        incs = [_dot_tn(v[chunk_rows(c)], k_state[chunk_rows(c)]) for c in range(nch)]
        return s, incs, q_inter, v, [jnp.exp(r) for r in last_rows]

    def state_stage(hd, s, incs, q_inter, v, decays):
        o_intra = _dot(jnp.where(mask, s, 0.0).astype(BF16), v)
        states = [st_sc[hd]]
        for c in range(nch):
            states.append(decays[c] * states[c] + incs[c])
        st_sc[hd] = states[nch]
        inter = [_dot_nt(q_inter[chunk_rows(c)], states[c].astype(BF16)) for c in range(nch)]
        return o_intra, inter

    def output_stage(rb, hd, o_intra, inter):
        rows = slice(rb * blk, (rb + 1) * blk)
        cs = slice(hd * dk, (hd + 1) * dk)
        o = jnp.concatenate([o_intra[chunk_rows(c)] + inter[c] for c in range(nch)], axis=0)
        g = proj_sc[rows, 3 * w + hd * dk:3 * w + (hd + 1) * dk]
        y_sc[rows, cs] = (_rms(o) * (g * _sigmoid(g) * ng_ref[:, cs])).astype(BF16)

    items = [(rb, hd) for rb in range(ts // blk) for hd in range(heads)]
    stage1, stage2 = {}, {}
    for n in range(len(items) + 2):
        if n < len(items):
            stage1[n] = scores_stage(*items[n])
        if 1 <= n <= len(items):
            stage2[n - 1] = state_stage(items[n - 1][1], *stage1.pop(n - 1))
        if n >= 2:
            output_stage(*items[n - 2], *stage2.pop(n - 2))

    o_ref[0] = x + mod[2:3] * _dot(y_sc[...], wout_ref[...])


def _hgrn_call(x, mod, w_in, w_out, lb_logits, layer, norm_g):
    bsz, seq, d = x.shape
    ts = HGRN_TS
    heads = w_out.shape[0] // HEAD_DIM
    w = heads * HEAD_DIM
    n_lb = lb_logits.shape[0]
    idx = jnp.arange(HGRN_BLK)
    tri = ((idx[:, None] >= idx[None, :]) & (idx[:, None] // HGRN_CHUNK == idx[None, :] // HGRN_CHUNK)).astype(BF16)
    kern = functools.partial(_hgrn_kernel, ts=ts, heads=heads, layer=layer)
    return pl.pallas_call(
        kern,
        grid=(bsz, seq // ts),
        in_specs=[
            pl.BlockSpec((1, ts, d), lambda b, s: (b, s, 0)),
            pl.BlockSpec((1, 6, d), lambda b, s: (b, 0, 0)),
            _resident((d, 4 * w), lambda b, s: (0, 0)),
            _resident((w, d), lambda b, s: (0, 0)),
            _resident((n_lb, w), lambda b, s: (0, 0)),
            _resident((1, w), lambda b, s: (0, 0)),
            _resident((HGRN_BLK, HGRN_BLK), lambda b, s: (0, 0)),
        ],
        out_specs=pl.BlockSpec((1, ts, d), lambda b, s: (b, s, 0)),
        out_shape=jax.ShapeDtypeStruct(x.shape, F32),
        scratch_shapes=[
            pltpu.VMEM((ts, 4 * w), F32),
            pltpu.VMEM((ts, w), F32),
            pltpu.VMEM((ts, w), BF16),
            pltpu.VMEM((heads, HEAD_DIM, HEAD_DIM), F32),
        ],
        compiler_params=_params(("arbitrary", "arbitrary")),
        name="hgrn2_layer",
    )(x, mod, w_in.astype(BF16), w_out.astype(BF16), lb_logits.astype(F32),
      jnp.tile(norm_g, heads).reshape(1, w), tri)


def _ffn_kernel(x_ref, mod_ref, wup_ref, cw_ref, cb_ref, wdn_ref, o_ref, tail_sc, perm_sc, act_sc, *, tm, ff):
    fc = FFN_FC
    r = SUBLANES
    d = x_ref.shape[-1]
    nj = tm // r
    n_slab = d // LANES

    @pl.when(pl.program_id(1) == 0)
    def _():
        tail_sc[...] = jnp.zeros_like(tail_sc)

    def perm_rows(jn):
        g = nj // r
        return pl.ds((jn % g) * r * r + jn // g, r, stride=r)

    x = x_ref[0]
    mod = mod_ref[0]
    h_nat = _modulate(x, mod[3:4], mod[4:5])
    for sl in range(n_slab):
        for jn in range(nj):
            perm_sc[sl, perm_rows(jn), :] = h_nat[r * jn:r * jn + r, sl * LANES:(sl + 1) * LANES]
    h = jnp.concatenate([perm_sc[sl] for sl in range(n_slab)], axis=1).astype(BF16)

    first_row = lax.broadcasted_iota(jnp.int32, (r, fc), 0) == 0

    def conv_cols(c0):
        u = _dot(h, wup_ref[:, c0:c0 + fc])
        prev = tail_sc[:, c0:c0 + fc]
        tail_sc[:, c0:c0 + fc] = u[tm - 2 * r:tm]
        back1 = jnp.where(first_row, pltpu.roll(prev[r:2 * r], 1, 0), pltpu.roll(u[tm - r:tm], 1, 0))
        back2 = jnp.where(first_row, pltpu.roll(prev[0:r], 1, 0), pltpu.roll(u[tm - 2 * r:tm - r], 1, 0))
        u1 = jnp.concatenate([back1, u[0:tm - r]], axis=0)
        u2 = jnp.concatenate([back2, back1, u[0:tm - 2 * r]], axis=0)
        cw = cw_ref[:, c0:c0 + fc]
        return cw[2:3] * u + cw[1:2] * u1 + cw[0:1] * u2 + cb_ref[:, c0:c0 + fc]

    for j in range(ff // fc):
        gate = conv_cols(j * fc)
        val = conv_cols(ff + j * fc)
        act_sc[:, j * fc:(j + 1) * fc] = (gate * _sigmoid(gate) * val).astype(BF16)
    y = _dot(act_sc[...], wdn_ref[...])

    for sl in range(n_slab):
        perm_sc[sl] = y[:, sl * LANES:(sl + 1) * LANES]
    y_nat = jnp.concatenate(
        [jnp.concatenate([perm_sc[sl, perm_rows(jn), :] for jn in range(nj)], axis=0) for sl in range(n_slab)],
        axis=1)
    o_ref[0] = x + mod[5:6] * y_nat


def _ffn_call(x, mod, w_up, conv_w, conv_b, w_down, layer):
    bsz, seq, d = x.shape
    ff = w_down.shape[1]
    tm = FFN_TM
    kern = functools.partial(_ffn_kernel, tm=tm, ff=ff)
    return pl.pallas_call(
        kern,
        grid=(bsz, seq // tm),
        in_specs=[
            pl.BlockSpec((1, tm, d), lambda b, s: (b, s, 0)),
            pl.BlockSpec((1, 6, d), lambda b, s: (b, 0, 0)),
            _resident((None, d, 2 * ff), lambda b, s: (layer, 0, 0)),
            _resident((None, CONV_W, 2 * ff), lambda b, s: (layer, 0, 0)),
            _resident((None, 1, 2 * ff), lambda b, s: (layer, 0, 0)),
            _resident((None, ff, d), lambda b, s: (layer, 0, 0)),
        ],
        out_specs=pl.BlockSpec((1, tm, d), lambda b, s: (b, s, 0)),
        out_shape=jax.ShapeDtypeStruct(x.shape, F32),
        scratch_shapes=[
            pltpu.VMEM((2 * SUBLANES, 2 * ff), F32),
            pltpu.VMEM((d // LANES, tm, LANES), F32),
            pltpu.VMEM((tm, ff), BF16),
        ],
        compiler_params=_params(("arbitrary", "arbitrary")),
        name="conv_glu_ffn",
    )(x, mod, w_up, conv_w, conv_b, w_down)


def _fox_kernel(x_ref, mod_ref, kvmod_ref, wkv_ref, wf_ref, bf_ref, kg_ref, tri_ref, wq_ref, qg_ref, wout_ref, o_ref,
                q_sc, gate_sc, y_sc, m_sc, acc_sc, k_all, v_all, f_all, carry_sc, *, tq, heads):
    dh = HEAD_DIM
    w = heads * dh
    qi = pl.program_id(1)
    q0 = pl.multiple_of(qi * tq, tq)
    tile = pl.ds(q0, tq)

    @pl.when(qi == 0)
    def _():
        carry_sc[...] = jnp.zeros_like(carry_sc)

    x = x_ref[0]
    mod = mod_ref[0]
    kvmod = kvmod_ref[0]
    xn = _rms(x)
    hk = (xn * (1.0 + kvmod[1:2]) + kvmod[0:1]).astype(BF16)
    h = (xn * (1.0 + mod[1:2]) + mod[0:1]).astype(BF16)

    kg = kg_ref[...]
    half = w // 2

    def keys(c0):
        kk = _dot(hk, wkv_ref[:, c0:c0 + half])
        for hd in range(half // dh):
            cs = slice(hd * dh, (hd + 1) * dh)
            k_all[tile, c0 + hd * dh:c0 + (hd + 1) * dh] = (_rms(kk[:, cs]) * kg).astype(BF16)

    def values(c0):
        v_all[tile, c0:c0 + half] = _dot(hk, wkv_ref[:, w + c0:w + c0 + half]).astype(BF16)

    z = _dot(hk, wf_ref[...]) + bf_ref[...]
    keys(0)
    log_f = jnp.minimum(z, 0.0) - jnp.log(1.0 + jnp.exp(-jnp.abs(z)))
    hi, lo = _split_bf16(log_f)
    tri = tri_ref[...]
    cum = _dot(tri, hi) + _dot(tri, lo) + carry_sc[0:1, :]
    keys(half)
    carry_sc[...] = jnp.broadcast_to(cum[tq - 1:tq, :], carry_sc.shape)
    values(0)
    t0 = -LOG2E * cum
    b0 = t0.astype(BF16).astype(F32)
    t1 = t0 - b0
    b1 = t1.astype(BF16).astype(F32)
    b2 = t1 - b1
    sub = lax.broadcasted_iota(jnp.int32, (tq, dh), 1) & (BIAS_LANES - 1)
    f_all[tile, :] = jnp.where(sub == 0, b0, jnp.where(sub == 1, b1, jnp.where(sub == 2, b2, 0.0))).astype(BF16)
    values(half)

    pq = _dot(h, wq_ref[...])
    gate_sc[...] = _sigmoid(pq[:, w:2 * w])
    qg = qg_ref[...] * (dh ** -0.5 * LOG2E)
    lane = lax.broadcasted_iota(jnp.int32, (tq, dh), 1)
    for hd in range(heads):
        cs = slice(hd * dh, (hd + 1) * dh)
        q_sc[hd, :, 0:dh] = (_rms(pq[:, cs]) * qg).astype(BF16)
        picks_bias = (lane - hd * BIAS_LANES).astype(jnp.uint32) < 3
        q_sc[hd, :, dh:2 * dh] = jnp.where(picks_bias, 1.0, 0.0).astype(BF16)
    m_sc[...] = jnp.full(m_sc.shape, NEG_INF, F32)
    acc_sc[...] = jnp.zeros(acc_sc.shape, F32)

    def run_tiles(tiles):
        ctx = []
        for ks, width, row0, masked in tiles:
            nr = tq - row0
            causal = None
            if masked:
                row = lax.broadcasted_iota(jnp.int32, (nr, width), 0)
                col = lax.broadcasted_iota(jnp.int32, (nr, width), 1)
                causal = col <= row
            ctx.append((ks, width, slice(row0, tq), nr, causal, f_all[pl.ds(ks, width), :], jnp.ones((width, dh), BF16)))

        def logits(t, hd):
            ks, width, rows, _, _, f_tile, _ = ctx[t]
            k_aug = jnp.concatenate([k_all[pl.ds(ks, width), hd * dh:(hd + 1) * dh], f_tile], axis=1)
            return _dot_nt(q_sc[hd, rows, :], k_aug)

        def weights(t, hd, s):
            ks, width, rows, nr, causal, _, ones = ctx[t]
            if causal is not None:
                s = jnp.where(causal, s, NEG_INF)
            m_prev = m_sc[hd, rows, :]
            m_new = jnp.maximum(m_prev, jnp.broadcast_to(jnp.max(s, axis=-1, keepdims=True), (nr, dh)))
            m_sc[hd, rows, :] = m_new
            alpha = jnp.exp2(m_prev - m_new)
            p = jnp.exp2(s - jnp.concatenate([m_new] * (width // dh), axis=1)).astype(BF16)
            v_aug = jnp.concatenate([v_all[pl.ds(ks, width), hd * dh:(hd + 1) * dh], ones], axis=1)
            return alpha, _dot(p, v_aug)

        def accumulate(t, hd, alpha, pv):
            rows = ctx[t][2]
            acc_sc[hd, rows, :] = jnp.concatenate([alpha, alpha], axis=1) * acc_sc[hd, rows, :] + pv

        items = [(t, hd) for t in range(len(tiles)) for hd in range(heads)]
        stage1, stage2 = {}, {}
        for n in range(len(items) + 2):
            if n < len(items):
                stage1[n] = logits(*items[n])
            if 1 <= n <= len(items):
                stage2[n - 1] = weights(*items[n - 1], stage1.pop(n - 1))
            if n >= 2:
                accumulate(*items[n - 2], *stage2.pop(n - 2))

    def full_tile(j):
        return (pl.multiple_of(j * tq, tq), tq, 0, False)

    def pair_body(jj, carry):
        run_tiles([full_tile(2 * jj), full_tile(2 * jj + 1)])
        return carry

    lax.fori_loop(0, qi // 2, pair_body, 0)

    @pl.when(qi % 2 == 1)
    def _():
        run_tiles([full_tile(qi - 1)])

    run_tiles([(pl.multiple_of(q0 + r * ATT_DIAG, ATT_DIAG), ATT_DIAG, r * ATT_DIAG, True)
               for r in range(tq // ATT_DIAG)])

    for hd in range(heads):
        cs = slice(hd * dh, (hd + 1) * dh)
        acc = acc_sc[hd]
        y_sc[:, cs] = (acc[:, 0:dh] / acc[:, dh:2 * dh] * gate_sc[:, cs]).astype(BF16)

    o_ref[0] = x + mod[2:3] * _dot(y_sc[...], wout_ref[...])


def _fox_call(x, mod, kv_mod, kv_w, kv_b_f, k_norm_g, w_q, q_norm_g, w_out):
    bsz, seq, d = x.shape
    w = w_out.shape[0]
    heads = w // HEAD_DIM
    tq = ATT_TQ
    assert heads * BIAS_LANES <= HEAD_DIM
    idx = jnp.arange(tq)
    tri = (idx[:, None] >= idx[None, :]).astype(BF16)
    lane = jnp.arange(HEAD_DIM)
    head_of_lane = jnp.minimum(lane // BIAS_LANES, heads - 1)
    used = (lane // BIAS_LANES < heads) & (lane % BIAS_LANES < 3)
    wf = jnp.where(used[None, :], kv_w[:, 2 * w:][:, head_of_lane], 0.0).astype(BF16)
    bf = jnp.where(used, kv_b_f.astype(F32)[head_of_lane], 0.0).reshape(1, HEAD_DIM)
    kern = functools.partial(_fox_kernel, tq=tq, heads=heads)
    return pl.pallas_call(
        kern,
        grid=(bsz, seq // tq),
        in_specs=[
            pl.BlockSpec((1, tq, d), lambda b, s: (b, s, 0)),
            pl.BlockSpec((1, 6, d), lambda b, s: (b, 0, 0)),
            pl.BlockSpec((1, 2, d), lambda b, s: (b, 0, 0)),
            _resident((d, 2 * w), lambda b, s: (0, 0)),
            _resident((d, HEAD_DIM), lambda b, s: (0, 0)),
            _resident((1, HEAD_DIM), lambda b, s: (0, 0)),
            _resident((1, HEAD_DIM), lambda b, s: (0, 0)),
            _resident((tq, tq), lambda b, s: (0, 0)),
            _resident((d, 2 * w), lambda b, s: (0, 0)),
            _resident((1, HEAD_DIM), lambda b, s: (0, 0)),
            _resident((w, d), lambda b, s: (0, 0)),
        ],
        out_specs=pl.BlockSpec((1, tq, d), lambda b, s: (b, s, 0)),
        out_shape=jax.ShapeDtypeStruct(x.shape, F32),
        scratch_shapes=[
            pltpu.VMEM((heads, tq, 2 * HEAD_DIM), BF16),
            pltpu.VMEM((tq, w), F32),
            pltpu.VMEM((tq, w), BF16),
            pltpu.VMEM((heads, tq, HEAD_DIM), F32),
            pltpu.VMEM((heads, tq, 2 * HEAD_DIM), F32),
            pltpu.VMEM((seq, w), BF16),
            pltpu.VMEM((seq, w), BF16),
            pltpu.VMEM((seq, HEAD_DIM), BF16),
            pltpu.VMEM((SUBLANES, HEAD_DIM), F32),
        ],
        compiler_params=_params(("arbitrary", "arbitrary")),
        name="fox_layer",
    )(x, mod, kv_mod, kv_w[:, :2 * w].astype(BF16), wf, bf, k_norm_g.reshape(1, HEAD_DIM), tri,
      w_q.astype(BF16), q_norm_g.reshape(1, HEAD_DIM), w_out.astype(BF16))


def kernel(x, c, ada_w, ada_b, a_w_in, a_lb_logits, a_norm_g, a_w_out, kv_ada_w, kv_ada_b, kv_w, kv_b_f, k_norm_g,
           b_w_q, q_norm_g, b_w_out, ffn_w_up, ffn_conv_w, ffn_conv_b, ffn_w_down):
    bsz, seq, d = x.shape
    depth = ada_w.shape[0]
    n_a = a_w_in.shape[0]
    ada_b3 = ada_b.reshape(depth, 1, 6 * d)
    assert depth - n_a == 1, "fox_layer computes the shared K/V in-kernel: exactly one FoX layer is supported"
    ffn_up = ffn_w_up.astype(BF16)
    ffn_down = ffn_w_down.astype(BF16)
    ffn_cb = ffn_conv_b.reshape(depth, 1, ffn_conv_b.shape[-1])
    for l in range(depth):
        mod = _mod_call(c, ada_w, ada_b3, l).reshape(bsz, 6, d)
        if l < n_a:
            x = _hgrn_call(x, mod, a_w_in[l], a_w_out[l], a_lb_logits, l, a_norm_g[l])
        else:
            j = l - n_a
            kv_mod = _mod_call(c, kv_ada_w[None], kv_ada_b.reshape(1, 1, 2 * d), 0).reshape(bsz, 2, d)
            x = _fox_call(x, mod, kv_mod, kv_w, kv_b_f, k_norm_g, b_w_q[j], q_norm_g[j], b_w_out[j])
        x = _ffn_call(x, mod, ffn_up, ffn_conv_w, ffn_cb, ffn_down, l)
    return x
```

```python
import functools

import jax
import jax.numpy as jnp
from jax import lax
from jax.experimental import pallas as pl
from jax.experimental.pallas import tpu as pltpu

F32 = jnp.float32
BF16 = jnp.bfloat16

EPS = 1e-6
NEG_INF = -1e30
HEAD_DIM = 128
HGRN_CHUNK = 64
CONV_W = 3
LOG2E = 1.4426950408889634
BIAS_LANES = 16

V7X_VMEM_BYTES = 64 * 1024 * 1024
VMEM_LIMIT_BYTES = V7X_VMEM_BYTES - 8 * 1024 * 1024
SUBLANES = 8
LANES = 128

MOD_TN = 1024
HGRN_TS = 1024
HGRN_BLK = 128
FFN_TM = 1024
FFN_FC = 256
ATT_TQ = 512
ATT_DIAG = 256


def _dot(a, b):
    return jnp.dot(a, b, preferred_element_type=F32)


def _dot_nt(a, b):
    return lax.dot_general(a, b, (((1,), (1,)), ((), ())), preferred_element_type=F32)


def _dot_tn(a, b):
    return lax.dot_general(a, b, (((0,), (0,)), ((), ())), preferred_element_type=F32)


def _sigmoid(x):
    return 1.0 / (1.0 + jnp.exp(-x))


def _rms(x):
    return x * lax.rsqrt(jnp.mean(x * x, axis=-1, keepdims=True) + EPS)


def _modulate(x, shift, scale):
    return _rms(x) * (1.0 + scale) + shift


def _split_bf16(x):
    hi = x.astype(BF16)
    lo = (x - hi.astype(F32)).astype(BF16)
    return hi, lo


def _resident(block_shape, index_map):
    return pl.BlockSpec(block_shape, index_map, pipeline_mode=pl.Buffered(1))


def _params(semantics):
    return pltpu.CompilerParams(dimension_semantics=semantics, vmem_limit_bytes=VMEM_LIMIT_BYTES)


def _mod_kernel(c_ref, w_ref, b_ref, o_ref):
    c = c_ref[...]
    ca_hi, ca_lo = _split_bf16(c * _sigmoid(c))
    w_hi, w_lo = _split_bf16(w_ref[...])
    acc = _dot(ca_hi, w_hi) + _dot(ca_lo, w_hi) + _dot(ca_hi, w_lo)
    o_ref[...] = acc + b_ref[...]


def _mod_call(c, w3, b3, layer):
    bsz, d = c.shape
    n = w3.shape[-1]
    tn = MOD_TN
    return pl.pallas_call(
        _mod_kernel,
        grid=(n // tn,),
        in_specs=[
            pl.BlockSpec((bsz, d), lambda j: (0, 0)),
            pl.BlockSpec((None, d, tn), lambda j: (layer, 0, j)),
            pl.BlockSpec((None, 1, tn), lambda j: (layer, 0, j)),
        ],
        out_specs=pl.BlockSpec((bsz, tn), lambda j: (0, j)),
        out_shape=jax.ShapeDtypeStruct((bsz, n), F32),
        compiler_params=_params(("arbitrary",)),
        name="adaln_mod",
    )(c, w3, b3)


def _hgrn_kernel(x_ref, mod_ref, win_ref, wout_ref, lb_ref, ng_ref, tri_ref, o_ref,
                 proj_sc, bc_sc, y_sc, st_sc, *, ts, heads, layer):
    dk = HEAD_DIM
    w = heads * dk
    chunk = HGRN_CHUNK
    blk = HGRN_BLK
    nch = blk // chunk

    @pl.when(pl.program_id(1) == 0)
    def _():
        st_sc[...] = jnp.zeros_like(st_sc)

    x = x_ref[0]
    mod = mod_ref[0]
    h = _modulate(x, mod[0:1], mod[1:2]).astype(BF16)
    proj_sc[...] = _dot(h, win_ref[...])

    lb_e = jnp.exp(lb_ref[...] - jnp.max(lb_ref[...], axis=0, keepdims=True))
    lb = jnp.sum(lb_e[0:layer + 1], axis=0, keepdims=True) / jnp.sum(lb_e, axis=0, keepdims=True)
    fg = lb + (1.0 - lb) * _sigmoid(proj_sc[:, w:2 * w])
    proj_sc[:, w:2 * w] = fg
    logf_hi, logf_lo = _split_bf16(jnp.log(fg))
    tri = tri_ref[...]
    for rb in range(ts // blk):
        rows = slice(rb * blk, (rb + 1) * blk)
        bc_sc[rows] = _dot(tri, logf_hi[rows]) + _dot(tri, logf_lo[rows])

    row = lax.broadcasted_iota(jnp.int32, (blk, blk), 0)
    col = lax.broadcasted_iota(jnp.int32, (blk, blk), 1)
    mask = (row - col).astype(jnp.uint32) <= (row & (chunk - 1)).astype(jnp.uint32)

    def per_chunk(rows_1):
        return jnp.concatenate([jnp.broadcast_to(r, (chunk, dk)) for r in rows_1], axis=0)

    def chunk_rows(c):
        return slice(c * chunk, (c + 1) * chunk)

    def scores_stage(rb, hd):
        rows = slice(rb * blk, (rb + 1) * blk)
        bc = bc_sc[rows, hd * dk:(hd + 1) * dk]
        mid_rows = [bc[c * chunk + chunk // 2:c * chunk + chunk // 2 + 1] for c in range(nch)]
        last_rows = [bc[(c + 1) * chunk - 1:(c + 1) * chunk] for c in range(nch)]
        b_mid = per_chunk(mid_rows)
        q = proj_sc[rows, hd * dk:(hd + 1) * dk]
        q_mid = q * _sigmoid(q) * jnp.exp(bc - b_mid)
        k_mid = (1.0 - proj_sc[rows, w + hd * dk:w + (hd + 1) * dk]) * jnp.exp(b_mid - bc)
        q_inter = (q_mid * per_chunk([jnp.exp(r) for r in mid_rows])).astype(BF16)
        k_state = (k_mid * per_chunk([jnp.exp(l - r) for l, r in zip(last_rows, mid_rows)])).astype(BF16)
        v = proj_sc[rows, 2 * w + hd * dk:2 * w + (hd + 1) * dk].astype(BF16)
        s = _dot_nt(q_mid.astype(BF16), k_mid.astype(BF16))
        incs = [_dot_tn(v[chunk_rows(c)], k_state[chunk_rows(c)]) for c in range(nch)]
        return s, incs, q_inter, v, [jnp.exp(r) for r in last_rows]

    def state_stage(hd, s, incs, q_inter, v, decays):
        o_intra = _dot(jnp.where(mask, s, 0.0).astype(BF16), v)
        states = [st_sc[hd]]
        for c in range(nch):
            states.append(decays[c] * states[c] + incs[c])
        st_sc[hd] = states[nch]
        inter = [_dot_nt(q_inter[chunk_rows(c)], states[c].astype(BF16)) for c in range(nch)]
        return o_intra, inter

    def output_stage(rb, hd, o_intra, inter):
        rows = slice(rb * blk, (rb + 1) * blk)
        cs = slice(hd * dk, (hd + 1) * dk)
        o = jnp.concatenate([o_intra[chunk_rows(c)] + inter[c] for c in range(nch)], axis=0)
        g = proj_sc[rows, 3 * w + hd * dk:3 * w + (hd + 1) * dk]
        y_sc[rows, cs] = (_rms(o) * (g * _sigmoid(g) * ng_ref[:, cs])).astype(BF16)

    items = [(rb, hd) for rb in range(ts // blk) for hd in range(heads)]
    stage1, stage2 = {}, {}
    for n in range(len(items) + 2):
        if n < len(items):
            stage1[n] = scores_stage(*items[n])
        if 1 <= n <= len(items):
            stage2[n - 1] = state_stage(items[n - 1][1], *stage1.pop(n - 1))
        if n >= 2:
            output_stage(*items[n - 2], *stage2.pop(n - 2))

    o_ref[0] = x + mod[2:3] * _dot(y_sc[...], wout_ref[...])


def _hgrn_call(x, mod, w_in, w_out, lb_logits, layer, norm_g):
    bsz, seq, d = x.shape
    ts = HGRN_TS
    heads = w_out.shape[0] // HEAD_DIM
    w = heads * HEAD_DIM
    n_lb = lb_logits.shape[0]
    idx = jnp.arange(HGRN_BLK)
    tri = ((idx[:, None] >= idx[None, :]) & (idx[:, None] // HGRN_CHUNK == idx[None, :] // HGRN_CHUNK)).astype(BF16)
    kern = functools.partial(_hgrn_kernel, ts=ts, heads=heads, layer=layer)
    return pl.pallas_call(
        kern,
        grid=(bsz, seq // ts),
        in_specs=[
            pl.BlockSpec((1, ts, d), lambda b, s: (b, s, 0)),
            pl.BlockSpec((1, 6, d), lambda b, s: (b, 0, 0)),
            _resident((d, 4 * w), lambda b, s: (0, 0)),
            _resident((w, d), lambda b, s: (0, 0)),
            _resident((n_lb, w), lambda b, s: (0, 0)),
            _resident((1, w), lambda b, s: (0, 0)),
            _resident((HGRN_BLK, HGRN_BLK), lambda b, s: (0, 0)),
        ],
        out_specs=pl.BlockSpec((1, ts, d), lambda b, s: (b, s, 0)),
        out_shape=jax.ShapeDtypeStruct(x.shape, F32),
        scratch_shapes=[
            pltpu.VMEM((ts, 4 * w), F32),
            pltpu.VMEM((ts, w), F32),
            pltpu.VMEM((ts, w), BF16),
            pltpu.VMEM((heads, HEAD_DIM, HEAD_DIM), F32),
        ],
        compiler_params=_params(("arbitrary", "arbitrary")),
        name="hgrn2_layer",
    )(x, mod, w_in.astype(BF16), w_out.astype(BF16), lb_logits.astype(F32),
      jnp.tile(norm_g, heads).reshape(1, w), tri)


def _ffn_kernel(x_ref, mod_ref, wup_ref, cw_ref, cb_ref, wdn_ref, o_ref, tail_sc, perm_sc, act_sc, *, tm, ff):
    fc = FFN_FC
    r = SUBLANES
    d = x_ref.shape[-1]
    nj = tm // r
    n_slab = d // LANES

    @pl.when(pl.program_id(1) == 0)
    def _():
        tail_sc[...] = jnp.zeros_like(tail_sc)

    def perm_rows(jn):
        g = nj // r
        return pl.ds((jn % g) * r * r + jn // g, r, stride=r)

    x = x_ref[0]
    mod = mod_ref[0]
    h_nat = _modulate(x, mod[3:4], mod[4:5])
    for sl in range(n_slab):
        for jn in range(nj):
            perm_sc[sl, perm_rows(jn), :] = h_nat[r * jn:r * jn + r, sl * LANES:(sl + 1) * LANES]
    h = jnp.concatenate([perm_sc[sl] for sl in range(n_slab)], axis=1).astype(BF16)

    first_row = lax.broadcasted_iota(jnp.int32, (r, fc), 0) == 0

    def conv_cols(c0):
        u = _dot(h, wup_ref[:, c0:c0 + fc])
        prev = tail_sc[:, c0:c0 + fc]
        tail_sc[:, c0:c0 + fc] = u[tm - 2 * r:tm]
        back1 = jnp.where(first_row, pltpu.roll(prev[r:2 * r], 1, 0), pltpu.roll(u[tm - r:tm], 1, 0))
        back2 = jnp.where(first_row, pltpu.roll(prev[0:r], 1, 0), pltpu.roll(u[tm - 2 * r:tm - r], 1, 0))
        u1 = jnp.concatenate([back1, u[0:tm - r]], axis=0)
        u2 = jnp.concatenate([back2, back1, u[0:tm - 2 * r]], axis=0)
        cw = cw_ref[:, c0:c0 + fc]
        return cw[2:3] * u + cw[1:2] * u1 + cw[0:1] * u2 + cb_ref[:, c0:c0 + fc]

    for j in range(ff // fc):
        gate = conv_cols(j * fc)
        val = conv_cols(ff + j * fc)
        act_sc[:, j * fc:(j + 1) * fc] = (gate * _sigmoid(gate) * val).astype(BF16)
    y = _dot(act_sc[...], wdn_ref[...])

    for sl in range(n_slab):
        perm_sc[sl] = y[:, sl * LANES:(sl + 1) * LANES]
    y_nat = jnp.concatenate(
        [jnp.concatenate([perm_sc[sl, perm_rows(jn), :] for jn in range(nj)], axis=0) for sl in range(n_slab)],
        axis=1)
    o_ref[0] = x + mod[5:6] * y_nat


def _ffn_call(x, mod, w_up, conv_w, conv_b, w_down, layer):
    bsz, seq, d = x.shape
    ff = w_down.shape[1]
    tm = FFN_TM
    kern = functools.partial(_ffn_kernel, tm=tm, ff=ff)
    return pl.pallas_call(
        kern,
        grid=(bsz, seq // tm),
        in_specs=[
            pl.BlockSpec((1, tm, d), lambda b, s: (b, s, 0)),
            pl.BlockSpec((1, 6, d), lambda b, s: (b, 0, 0)),
            _resident((None, d, 2 * ff), lambda b, s: (layer, 0, 0)),
            _resident((None, CONV_W, 2 * ff), lambda b, s: (layer, 0, 0)),
            _resident((None, 1, 2 * ff), lambda b, s: (layer, 0, 0)),
            _resident((None, ff, d), lambda b, s: (layer, 0, 0)),
        ],
        out_specs=pl.BlockSpec((1, tm, d), lambda b, s: (b, s, 0)),
        out_shape=jax.ShapeDtypeStruct(x.shape, F32),
        scratch_shapes=[
            pltpu.VMEM((2 * SUBLANES, 2 * ff), F32),
            pltpu.VMEM((d // LANES, tm, LANES), F32),
            pltpu.VMEM((tm, ff), BF16),
        ],
        compiler_params=_params(("arbitrary", "arbitrary")),
        name="conv_glu_ffn",
    )(x, mod, w_up, conv_w, conv_b, w_down)


def _fox_kernel(x_ref, mod_ref, kvmod_ref, wkv_ref, wf_ref, bf_ref, kg_ref, tri_ref, wq_ref, qg_ref, wout_ref, o_ref,
                q_sc, gate_sc, y_sc, m_sc, acc_sc, k_all, v_all, f_all, carry_sc, *, tq, heads):
    dh = HEAD_DIM
    w = heads * dh
    qi = pl.program_id(1)
    q0 = pl.multiple_of(qi * tq, tq)
    tile = pl.ds(q0, tq)

    @pl.when(qi == 0)
    def _():
        carry_sc[...] = jnp.zeros_like(carry_sc)

    x = x_ref[0]
    mod = mod_ref[0]
    kvmod = kvmod_ref[0]
    xn = _rms(x)
    hk = (xn * (1.0 + kvmod[1:2]) + kvmod[0:1]).astype(BF16)
    h = (xn * (1.0 + mod[1:2]) + mod[0:1]).astype(BF16)

    kg = kg_ref[...]
    half = w // 2

    def keys(c0):
        kk = _dot(hk, wkv_ref[:, c0:c0 + half])
        for hd in range(half // dh):
            cs = slice(hd * dh, (hd + 1) * dh)
            k_all[tile, c0 + hd * dh:c0 + (hd + 1) * dh] = (_rms(kk[:, cs]) * kg).astype(BF16)

    def values(c0):
        v_all[tile, c0:c0 + half] = _dot(hk, wkv_ref[:, w + c0:w + c0 + half]).astype(BF16)

    z = _dot(hk, wf_ref[...]) + bf_ref[...]
    keys(0)
    log_f = jnp.minimum(z, 0.0) - jnp.log(1.0 + jnp.exp(-jnp.abs(z)))
    hi, lo = _split_bf16(log_f)
    tri = tri_ref[...]
    cum = _dot(tri, hi) + _dot(tri, lo) + carry_sc[0:1, :]
    keys(half)
    carry_sc[...] = jnp.broadcast_to(cum[tq - 1:tq, :], carry_sc.shape)
    values(0)
    t0 = -LOG2E * cum
    b0 = t0.astype(BF16).astype(F32)
    t1 = t0 - b0
    b1 = t1.astype(BF16).astype(F32)
    b2 = t1 - b1
    sub = lax.broadcasted_iota(jnp.int32, (tq, dh), 1) & (BIAS_LANES - 1)
    f_all[tile, :] = jnp.where(sub == 0, b0, jnp.where(sub == 1, b1, jnp.where(sub == 2, b2, 0.0))).astype(BF16)
    values(half)

    pq = _dot(h, wq_ref[...])
    gate_sc[...] = _sigmoid(pq[:, w:2 * w])
    qg = qg_ref[...] * (dh ** -0.5 * LOG2E)
    lane = lax.broadcasted_iota(jnp.int32, (tq, dh), 1)
    for hd in range(heads):
        cs = slice(hd * dh, (hd + 1) * dh)
        q_sc[hd, :, 0:dh] = (_rms(pq[:, cs]) * qg).astype(BF16)
        picks_bias = (lane - hd * BIAS_LANES).astype(jnp.uint32) < 3
        q_sc[hd, :, dh:2 * dh] = jnp.where(picks_bias, 1.0, 0.0).astype(BF16)
    m_sc[...] = jnp.full(m_sc.shape, NEG_INF, F32)
    acc_sc[...] = jnp.zeros(acc_sc.shape, F32)

    def run_tiles(tiles):
        ctx = []
        for ks, width, row0, masked in tiles:
            nr = tq - row0
            causal = None
            if masked:
                row = lax.broadcasted_iota(jnp.int32, (nr, width), 0)
                col = lax.broadcasted_iota(jnp.int32, (nr, width), 1)
                causal = col <= row
            ctx.append((ks, width, slice(row0, tq), nr, causal, f_all[pl.ds(ks, width), :], jnp.ones((width, dh), BF16)))

        def logits(t, hd):
            ks, width, rows, _, _, f_tile, _ = ctx[t]
            k_aug = jnp.concatenate([k_all[pl.ds(ks, width), hd * dh:(hd + 1) * dh], f_tile], axis=1)
            return _dot_nt(q_sc[hd, rows, :], k_aug)

        def weights(t, hd, s):
            ks, width, rows, nr, causal, _, ones = ctx[t]
            if causal is not None:
                s = jnp.where(causal, s, NEG_INF)
            m_prev = m_sc[hd, rows, :]
            m_new = jnp.maximum(m_prev, jnp.broadcast_to(jnp.max(s, axis=-1, keepdims=True), (nr, dh)))
            m_sc[hd, rows, :] = m_new
            alpha = jnp.exp2(m_prev - m_new)
            p = jnp.exp2(s - jnp.concatenate([m_new] * (width // dh), axis=1)).astype(BF16)
            v_aug = jnp.concatenate([v_all[pl.ds(ks, width), hd * dh:(hd + 1) * dh], ones], axis=1)
            return alpha, _dot(p, v_aug)

        def accumulate(t, hd, alpha, pv):
            rows = ctx[t][2]
            acc_sc[hd, rows, :] = jnp.concatenate([alpha, alpha], axis=1) * acc_sc[hd, rows, :] + pv

        items = [(t, hd) for t in range(len(tiles)) for hd in range(heads)]
        stage1, stage2 = {}, {}
        for n in range(len(items) + 2):
            if n < len(items):
                stage1[n] = logits(*items[n])
            if 1 <= n <= len(items):
                stage2[n - 1] = weights(*items[n - 1], stage1.pop(n - 1))
            if n >= 2:
                accumulate(*items[n - 2], *stage2.pop(n - 2))

    def full_tile(j):
        return (pl.multiple_of(j * tq, tq), tq, 0, False)

    def pair_body(jj, carry):
        run_tiles([full_tile(2 * jj), full_tile(2 * jj + 1)])
        return carry

    lax.fori_loop(0, qi // 2, pair_body, 0)

    @pl.when(qi % 2 == 1)
    def _():
        run_tiles([full_tile(qi - 1)])

    run_tiles([(pl.multiple_of(q0 + r * ATT_DIAG, ATT_DIAG), ATT_DIAG, r * ATT_DIAG, True)
               for r in range(tq // ATT_DIAG)])

    for hd in range(heads):
        cs = slice(hd * dh, (hd + 1) * dh)
        acc = acc_sc[hd]
        y_sc[:, cs] = (acc[:, 0:dh] / acc[:, dh:2 * dh] * gate_sc[:, cs]).astype(BF16)

    o_ref[0] = x + mod[2:3] * _dot(y_sc[...], wout_ref[...])


def _fox_call(x, mod, kv_mod, kv_w, kv_b_f, k_norm_g, w_q, q_norm_g, w_out):
    bsz, seq, d = x.shape
    w = w_out.shape[0]
    heads = w // HEAD_DIM
    tq = ATT_TQ
    assert heads * BIAS_LANES <= HEAD_DIM
    idx = jnp.arange(tq)
    tri = (idx[:, None] >= idx[None, :]).astype(BF16)
    lane = jnp.arange(HEAD_DIM)
    head_of_lane = jnp.minimum(lane // BIAS_LANES, heads - 1)
    used = (lane // BIAS_LANES < heads) & (lane % BIAS_LANES < 3)
    wf = jnp.where(used[None, :], kv_w[:, 2 * w:][:, head_of_lane], 0.0).astype(BF16)
    bf = jnp.where(used, kv_b_f.astype(F32)[head_of_lane], 0.0).reshape(1, HEAD_DIM)
    kern = functools.partial(_fox_kernel, tq=tq, heads=heads)
    return pl.pallas_call(
        kern,
        grid=(bsz, seq // tq),
        in_specs=[
            pl.BlockSpec((1, tq, d), lambda b, s: (b, s, 0)),
            pl.BlockSpec((1, 6, d), lambda b, s: (b, 0, 0)),
            pl.BlockSpec((1, 2, d), lambda b, s: (b, 0, 0)),
            _resident((d, 2 * w), lambda b, s: (0, 0)),
            _resident((d, HEAD_DIM), lambda b, s: (0, 0)),
            _resident((1, HEAD_DIM), lambda b, s: (0, 0)),
            _resident((1, HEAD_DIM), lambda b, s: (0, 0)),
            _resident((tq, tq), lambda b, s: (0, 0)),
            _resident((d, 2 * w), lambda b, s: (0, 0)),
            _resident((1, HEAD_DIM), lambda b, s: (0, 0)),
            _resident((w, d), lambda b, s: (0, 0)),
        ],
        out_specs=pl.BlockSpec((1, tq, d), lambda b, s: (b, s, 0)),
        out_shape=jax.ShapeDtypeStruct(x.shape, F32),
        scratch_shapes=[
            pltpu.VMEM((heads, tq, 2 * HEAD_DIM), BF16),
            pltpu.VMEM((tq, w), F32),
            pltpu.VMEM((tq, w), BF16),
            pltpu.VMEM((heads, tq, HEAD_DIM), F32),
            pltpu.VMEM((heads, tq, 2 * HEAD_DIM), F32),
            pltpu.VMEM((seq, w), BF16),
            pltpu.VMEM((seq, w), BF16),
            pltpu.VMEM((seq, HEAD_DIM), BF16),
            pltpu.VMEM((SUBLANES, HEAD_DIM), F32),
        ],
        compiler_params=_params(("arbitrary", "arbitrary")),
        name="fox_layer",
    )(x, mod, kv_mod, kv_w[:, :2 * w].astype(BF16), wf, bf, k_norm_g.reshape(1, HEAD_DIM), tri,
      w_q.astype(BF16), q_norm_g.reshape(1, HEAD_DIM), w_out.astype(BF16))


def kernel(x, c, ada_w, ada_b, a_w_in, a_lb_logits, a_norm_g, a_w_out, kv_ada_w, kv_ada_b, kv_w, kv_b_f, k_norm_g,
           b_w_q, q_norm_g, b_w_out, ffn_w_up, ffn_conv_w, ffn_conv_b, ffn_w_down):
    bsz, seq, d = x.shape
    depth = ada_w.shape[0]
    n_a = a_w_in.shape[0]
    ada_b3 = ada_b.reshape(depth, 1, 6 * d)
    assert depth - n_a == 1, "fox_layer computes the shared K/V in-kernel: exactly one FoX layer is supported"
    ffn_up = ffn_w_up.astype(BF16)
    ffn_down = ffn_w_down.astype(BF16)
    ffn_cb = ffn_conv_b.reshape(depth, 1, ffn_conv_b.shape[-1])
    for l in range(depth):
        mod = _mod_call(c, ada_w, ada_b3, l).reshape(bsz, 6, d)
        if l < n_a:
            x = _hgrn_call(x, mod, a_w_in[l], a_w_out[l], a_lb_logits, l, a_norm_g[l])
        else:
            j = l - n_a
            kv_mod = _mod_call(c, kv_ada_w[None], kv_ada_b.reshape(1, 1, 2 * d), 0).reshape(bsz, 2, d)
            x = _fox_call(x, mod, kv_mod, kv_w, kv_b_f, k_norm_g, b_w_q[j], q_norm_g[j], b_w_out[j])
        x = _ffn_call(x, mod, ffn_up, ffn_conv_w, ffn_cb, ffn_down, l)
    return x
```
